```python
import jax, jax.numpy as jnp
from jax import lax
import numpy as np

D_MODEL = 1024
BATCH = 4
SEQ = 4096
DEPTH = 4
DEC_BATCH = 128
DEC_SEQ = 8
PAST_LEN = 8192
PAGE_SIZE = 128

N_HEADS = 16
N_KV_HEADS = 4
HEAD_DIM = D_MODEL // N_HEADS
GROUP = N_HEADS // N_KV_HEADS
WINDOW = 128
ATTN_BLOCK = WINDOW
D_ATTN = N_HEADS * HEAD_DIM
D_KV = N_KV_HEADS * HEAD_DIM
REC_HEADS = 8
REC_DK = 128
REC_DV = D_MODEL // REC_HEADS
D_REC_K = REC_HEADS * REC_DK
D_REC_V = REC_HEADS * REC_DV
REC_CHUNK = 64
D_FF = 4 * D_MODEL
N_MOD = 6
EPS = 1e-6
MASK_VALUE = -1e30
IN_SIZES = (D_ATTN, D_KV, D_KV, D_REC_K, D_REC_K, D_REC_V, D_REC_V, D_MODEL, D_MODEL)
IN_COLS = sum(IN_SIZES)

kernel_name = 'hybrid_swa_sink_alibi_hgrn2_adaln_step'


def rms_norm(x, g):
    xf = x.astype(jnp.float32)
    y = xf * lax.rsqrt(jnp.mean(jnp.square(xf), axis=-1, keepdims=True) + EPS)
    return (y * g.astype(jnp.float32)).astype(x.dtype)


def alibi_slopes():
    return 2.0 ** (-8.0 * jnp.arange(1, N_HEADS + 1, dtype=jnp.float32) / N_HEADS)


def sink_attention(q, k, v, qpos, kpos, sinks):
    s = jnp.einsum('bnqkgd,bnskd->bnkgqs', q, k).astype(jnp.float32) * HEAD_DIM ** -0.5
    dist = qpos[:, :, None] - kpos[:, None, :]
    valid = (dist >= 0) & (dist <= WINDOW) & (kpos[:, None, :] >= 0)
    slopes = alibi_slopes().reshape(N_KV_HEADS, GROUP)[None, None, :, :, None, None]
    s = s - slopes * dist.astype(jnp.float32)[None, :, None, None]
    s = jnp.where(valid[None, :, None, None], s, MASK_VALUE)
    sink = sinks.astype(jnp.float32).reshape(N_KV_HEADS, GROUP)[None, None, :, :, None, None]
    m = jnp.maximum(jnp.max(s, axis=-1, keepdims=True), sink)
    p = jnp.exp(s - m)
    p = p / (jnp.sum(p, axis=-1, keepdims=True) + jnp.exp(sink - m))
    return jnp.einsum('bnkgqs,bnskd->bnqkgd', p.astype(v.dtype), v)


def swa_prompt(q, k, v, sinks):
    B, L = q.shape[:2]
    nb = L // ATTN_BLOCK
    qb = q.reshape(B, nb, ATTN_BLOCK, N_KV_HEADS, GROUP, HEAD_DIM)

    def with_prev(t):
        t = t.reshape(B, nb, ATTN_BLOCK, N_KV_HEADS, HEAD_DIM)
        prev = jnp.concatenate([jnp.zeros_like(t[:, :1]), t[:, :-1]], axis=1)
        return jnp.concatenate([prev, t], axis=2)

    start = jnp.arange(nb) * ATTN_BLOCK
    qpos = start[:, None] + jnp.arange(ATTN_BLOCK)[None]
    kpos = start[:, None] - ATTN_BLOCK + jnp.arange(2 * ATTN_BLOCK)[None]
    o = sink_attention(qb, with_prev(k), with_prev(v), qpos, kpos, sinks)
    return o.reshape(B, L, D_ATTN)


def swa_sample(q, k, v, buf_k, buf_v, sinks):
    B, L = q.shape[:2]
    w = buf_k.shape[1]
    k_all = jnp.concatenate([buf_k, k.astype(buf_k.dtype)], axis=1)
    v_all = jnp.concatenate([buf_v, v.astype(buf_v.dtype)], axis=1)
    qpos = (PAST_LEN + jnp.arange(L))[None]
    kpos = (PAST_LEN - w + jnp.arange(w + L))[None]
    o = sink_attention(q.reshape(B, 1, L, N_KV_HEADS, GROUP, HEAD_DIM),
                       k_all[:, None], v_all[:, None], qpos, kpos, sinks)
    return o.reshape(B, L, D_ATTN), k_all[:, -w:], v_all[:, -w:]


def hgrn2_chunked(q, k, v, logf, s0):
    B, L, H, _ = q.shape
    C = min(REC_CHUNK, L)
    n = -(-L // C)
    pad = n * C - L

    def prep(t):
        t = jnp.pad(t, ((0, 0), (0, pad), (0, 0), (0, 0)))
        return jnp.moveaxis(t.reshape(B, n, C, H, t.shape[-1]), 1, 0)

    tri = jnp.tril(jnp.ones((C, C), dtype=bool))[None, :, :, None, None]

    def step(S, inp):
        qc, kc, vc, gc = inp
        b = jnp.cumsum(gc, axis=1)
        o_inter = jnp.einsum('bthk,bhkv->bthv', qc * jnp.exp(b), S)
        diff = b[:, :, None] - b[:, None, :]
        decay = jnp.where(tri, jnp.exp(jnp.where(tri, diff, 0.0)), 0.0)
        att = jnp.einsum('bthk,bshk,btshk->bhts', qc, kc, decay)
        o_intra = jnp.einsum('bhts,bshv->bthv', att, vc)
        b_last = b[:, -1]
        S = jnp.exp(b_last)[..., None] * S + jnp.einsum(
            'bshk,bshv->bhkv', kc * jnp.exp(b_last[:, None] - b), vc)
        return S, o_inter + o_intra

    S, o = lax.scan(step, s0, (prep(q), prep(k), prep(v), prep(logf)))
    o = jnp.moveaxis(o, 0, 1).reshape(B, n * C, H, v.shape[-1])[:, :L]
    return o, S


def layer(x, c, prm, lb, s0, w_buf, buf_k=None, buf_v=None):
    (norm_mix, norm_mlp, w_ada, b_ada, w_in, sinks, hgrn_norm, w_out, w_up, w_down) = prm
    B, L, _ = x.shape
    mod = (jax.nn.silu(c) @ w_ada + b_ada)[:, None, :]
    sh_a, sc_a, g_a, sh_m, sc_m, g_m = jnp.split(mod, N_MOD, axis=-1)

    h = rms_norm(x, norm_mix) * (1.0 + sc_a) + sh_a
    z = h @ w_in
    q, k, v, rq, rf, ri, rg, ga, gr = jnp.split(
        z, np.cumsum(IN_SIZES)[:-1].tolist(), axis=-1)

    q = q.reshape(B, L, N_HEADS, HEAD_DIM)
    k = k.reshape(B, L, N_KV_HEADS, HEAD_DIM)
    v = v.reshape(B, L, N_KV_HEADS, HEAD_DIM)
    if buf_k is None:
        o_attn = swa_prompt(q, k, v, sinks)
        new_k, new_v = k[:, -w_buf:], v[:, -w_buf:]
    else:
        o_attn, new_k, new_v = swa_sample(q, k, v, buf_k, buf_v, sinks)

    fp = rf.astype(jnp.float32).reshape(B, L, REC_HEADS, REC_DK)
    lbh = lb.reshape(REC_HEADS, REC_DK)
    sig = jax.nn.sigmoid(fp)
    logf = jnp.log(lbh + (1.0 - lbh) * sig)
    kr = (1.0 - lbh) * (1.0 - sig)
    qr = jax.nn.silu(rq.astype(jnp.float32)).reshape(B, L, REC_HEADS, REC_DK) * REC_DK ** -0.5
    vr = ri.astype(jnp.float32).reshape(B, L, REC_HEADS, REC_DV)
    o_rec, s_new = hgrn2_chunked(qr, kr, vr, logf, s0.astype(jnp.float32))
    o_rec = rms_norm(o_rec, hgrn_norm.reshape(REC_HEADS, REC_DV)) * jax.nn.silu(
        rg.astype(jnp.float32).reshape(B, L, REC_HEADS, REC_DV))
    o_rec = o_rec.reshape(B, L, D_REC_V).astype(x.dtype)

    mixed = jax.nn.sigmoid(ga) * o_attn + jax.nn.sigmoid(gr) * o_rec
    x = x + g_a * (mixed @ w_out)

    h2 = rms_norm(x, norm_mlp) * (1.0 + sc_m) + sh_m
    x = x + g_m * (jnp.square(jax.nn.relu(h2 @ w_up)) @ w_down)
    return x, new_k, new_v, s_new.astype(s0.dtype)


def setup_inputs(seed: int = 0) -> dict:
    key = jax.random.key(seed)
    ks = jax.random.split(key, 22)
    w_buf = min(WINDOW, PAST_LEN)

    def nrm(k, shape, scale):
        return jax.random.normal(k, shape, jnp.float32) * scale

    return {
        'x_prompt': nrm(ks[0], (BATCH, SEQ, D_MODEL), 1.0),
        'x_sample': nrm(ks[1], (DEC_BATCH, DEC_SEQ, D_MODEL), 1.0),
        'cache_win_k': nrm(ks[2], (DEPTH, DEC_BATCH, w_buf, N_KV_HEADS, HEAD_DIM), 1.0),
        'cache_win_v': nrm(ks[3], (DEPTH, DEC_BATCH, w_buf, N_KV_HEADS, HEAD_DIM), 1.0),
        'state_hgrn': nrm(ks[4], (DEPTH, DEC_BATCH, REC_HEADS, REC_DK, REC_DV), 0.3),
        'c_prompt': nrm(ks[5], (BATCH, D_MODEL), 1.0),
        'c_sample': nrm(ks[6], (DEC_BATCH, D_MODEL), 1.0),
        'norm_mix': 1.0 + nrm(ks[7], (DEPTH, D_MODEL), 0.02),
        'norm_mlp': 1.0 + nrm(ks[8], (DEPTH, D_MODEL), 0.02),
        'w_ada': nrm(ks[9], (DEPTH, D_MODEL, N_MOD * D_MODEL), 0.5 * D_MODEL ** -0.5),
        'b_ada': nrm(ks[10], (DEPTH, N_MOD * D_MODEL), 0.02),
        'w_in': nrm(ks[11], (DEPTH, D_MODEL, IN_COLS), D_MODEL ** -0.5),
        'attn_sinks': nrm(ks[12], (DEPTH, N_HEADS), 0.5),
        'hgrn_lb_logits': nrm(ks[13], (DEPTH, D_REC_K), 0.5),
        'hgrn_norm': 1.0 + nrm(ks[14], (DEPTH, D_REC_V), 0.02),
        'w_out': nrm(ks[15], (DEPTH, D_MODEL, D_MODEL), D_MODEL ** -0.5),
        'w_up': nrm(ks[16], (DEPTH, D_MODEL, D_FF), D_MODEL ** -0.5),
        'w_down': nrm(ks[17], (DEPTH, D_FF, D_MODEL), D_FF ** -0.5),
        'final_norm': 1.0 + nrm(ks[18], (D_MODEL,), 0.02),
    }


def reference(x_prompt, x_sample, cache_win_k, cache_win_v, state_hgrn, c_prompt, c_sample,
              norm_mix, norm_mlp, w_ada, b_ada, w_in, attn_sinks, hgrn_lb_logits, hgrn_norm,
              w_out, w_up, w_down, final_norm):
    w_buf = cache_win_k.shape[2]
    p_lb = jax.nn.softmax(hgrn_lb_logits.astype(jnp.float32), axis=0)
    lb_all = jnp.cumsum(p_lb, axis=0) - p_lb[0]

    xp, xs = x_prompt, x_sample
    pk, pv, ps, sk, sv, ss = [], [], [], [], [], []
    for l in range(DEPTH):
        prm = (norm_mix[l], norm_mlp[l], w_ada[l], b_ada[l], w_in[l], attn_sinks[l],
               hgrn_norm[l], w_out[l], w_up[l], w_down[l])
        s0p = jnp.zeros((xp.shape[0], REC_HEADS, REC_DK, REC_DV), xp.dtype)
        xp, k_p, v_p, s_p = layer(xp, c_prompt, prm, lb_all[l], s0p, w_buf)
        xs, k_s, v_s, s_s = layer(xs, c_sample, prm, lb_all[l], state_hgrn[l], w_buf,
                                  cache_win_k[l], cache_win_v[l])
        pk.append(k_p); pv.append(v_p); ps.append(s_p)
        sk.append(k_s); sv.append(v_s); ss.append(s_s)

    y_prompt = rms_norm(xp, final_norm)
    y_sample = rms_norm(xs, final_norm)
    return (y_prompt, y_sample, jnp.stack(pk), jnp.stack(pv), jnp.stack(ps),
            jnp.stack(sk), jnp.stack(sv), jnp.stack(ss))
```

```python
import functools

import numpy as np
import jax
import jax.numpy as jnp
from jax import lax
from jax.experimental import pallas as pl
from jax.experimental.pallas import tpu as pltpu

f32 = jnp.float32
bf16 = jnp.bfloat16

D = 1024
N_HEADS = 16
N_KV = 4
HD = 64
WINDOW = 128
REC_HEADS = 8
DK = 128
D_FF = 4 * D
EPS = 1e-6
MASK_VALUE = -1e30
LANES = 128
SUBLANES = 8
VMEM_LIMIT = 56 * 1024 * 1024

T_ATT = 512
T_REC = 512
T_MLP = 512
CHUNK = 128
SB_ATT = 16
SB_REC = 8
SB_MLP = 64


def _cparams(n_axes):
    return pltpu.CompilerParams(
        dimension_semantics=("arbitrary",) * n_axes, vmem_limit_bytes=VMEM_LIMIT)


def _resident(shape, index_map):
    return pl.BlockSpec(shape, index_map, pipeline_mode=pl.Buffered(1))


def _dot(a, b):
    return jnp.dot(a, b, preferred_element_type=f32)


def _dot_nt(a, b):
    return lax.dot_general(a, b, (((1,), (1,)), ((), ())), preferred_element_type=f32)


def _dot_tn(a, b):
    return lax.dot_general(a, b, (((0,), (0,)), ((), ())), preferred_element_type=f32)


def _sigmoid(x):
    return 1.0 / (1.0 + jnp.exp(-x))


def _silu(x):
    return x * _sigmoid(x)


def _norm_mod(x, nw, sc, sh):
    ms = jnp.mean(x * x, axis=-1, keepdims=True)
    y = x * lax.rsqrt(ms + EPS) * nw
    return y * (1.0 + sc) + sh


def _mod_kernel(c_ref, w_ref, b_ref, o_ref):
    c = c_ref[...]
    s = _silu(c).astype(bf16)
    o_ref[0] = _dot(s, w_ref[0].astype(bf16)) + b_ref[0]


def _lb_kernel(l_ref, o_ref):
    rows = [l_ref[i:i + 1, :] for i in range(l_ref.shape[0])]
    m = functools.reduce(jnp.maximum, rows)
    e = [jnp.exp(r - m) for r in rows]
    tot = functools.reduce(lambda a, b: a + b, e)
    p = [x / tot for x in e]
    run = p[0]
    o_ref[0:1, :] = run - p[0]
    for i in range(1, len(p)):
        run = run + p[i]
        o_ref[i:i + 1, :] = run - p[0]


def _mlp_kernel(x_ref, mod_ref, nw_ref, wup_ref, wdn_ref, fn_ref, o_ref, *, final):
    x = x_ref[...]
    g, r, _ = x.shape
    sh = mod_ref[0, :, :, 3 * D:4 * D]
    sc = mod_ref[0, :, :, 4 * D:5 * D]
    gm = mod_ref[0, :, :, 5 * D:6 * D]
    h = _norm_mod(x, nw_ref[0], sc, sh).reshape(g * r, D).astype(bf16)
    acc = jnp.zeros((g * r, D), f32)
    for c in range(D_FF // D):
        u = _dot(h, wup_ref[0, :, c * D:(c + 1) * D])
        u = jnp.square(jnp.maximum(u, 0.0)).astype(bf16)
        acc = acc + _dot(u, wdn_ref[0, c * D:(c + 1) * D, :])
    y = x + gm * acc.reshape(g, r, D)
    if final:
        ms = jnp.mean(y * y, axis=-1, keepdims=True)
        y = y * lax.rsqrt(ms + EPS) * fn_ref[...]
    o_ref[...] = y


def _mlp_call(x, mod, mod_row0, nw, wup, wdn, fn, layer, final, gblk):
    ng, r, _ = x.shape
    grid = (ng // gblk,)
    mrow = mod_row0 // gblk
    return pl.pallas_call(
        functools.partial(_mlp_kernel, final=final),
        out_shape=jax.ShapeDtypeStruct(x.shape, f32),
        grid=grid,
        in_specs=[
            pl.BlockSpec((gblk, r, D), lambda i: (i, 0, 0)),
            pl.BlockSpec((1, gblk, 1, 6 * D), lambda i: (layer, mrow + i, 0, 0)),
            _resident((1, 1, D), lambda i: (layer, 0, 0)),
            _resident((1, D, D_FF), lambda i: (layer, 0, 0)),
            _resident((1, D_FF, D), lambda i: (layer, 0, 0)),
            _resident((1, D), lambda i: (0, 0)),
        ],
        out_specs=pl.BlockSpec((gblk, r, D), lambda i: (i, 0, 0)),
        compiler_params=_cparams(1),
        name="mlp",
    )(x, mod, nw, wup, wdn, fn)


def _lane_lt_half(shape):
    return lax.broadcasted_iota(jnp.int32, shape, len(shape) - 1) < HD


def _q_for_head(zq, h, lower):
    blk = zq[:, (h // 2) * LANES:(h // 2 + 1) * LANES]
    own, tgt = h % 2, (h // 4) % 2
    if own != tgt:
        blk = pltpu.roll(blk, HD, 1)
    keep = lower if tgt == 0 else jnp.logical_not(lower)
    return jnp.where(keep, blk, 0.0).astype(bf16)


def _heads_to_natural(o_heads, lower):
    cols = []
    for jn in range(N_HEADS // 2):
        tgt = (jn // 2) % 2
        a, b = o_heads[2 * jn], o_heads[2 * jn + 1]
        if tgt == 1:
            a = pltpu.roll(a, HD, 1)
        else:
            b = pltpu.roll(b, HD, 1)
        cols.append(jnp.where(lower, a, b))
    return jnp.concatenate(cols, axis=1)


def _softmax_parts(s, sink):
    m = jnp.maximum(jnp.max(s, axis=-1, keepdims=True), sink)
    p = jnp.exp(s - m)
    l = jnp.sum(p, axis=-1, keepdims=True) + jnp.exp(sink - m)
    return p, l


def _attn_prompt_kernel(sinks_ref, x_ref, mod_ref, nw_ref, w_ref, bias_ref,
                        a_ref, kw_ref, vw_ref, kprev_ref, vprev_ref, *, layer):
    t = pl.program_id(1)
    nt = pl.num_programs(1)

    @pl.when(t == 0)
    def _():
        kprev_ref[...] = jnp.zeros_like(kprev_ref)
        vprev_ref[...] = jnp.zeros_like(vprev_ref)

    x = x_ref[...]
    sh = mod_ref[0, :, :, 0:D]
    sc = mod_ref[0, :, :, D:2 * D]
    h = _norm_mod(x, nw_ref[0], sc, sh).reshape(T_ATT, D).astype(bf16)
    z = _dot(h, w_ref[0])
    zq = z[:, 0:D] * (HD ** -0.5)
    k = z[:, D:D + N_KV * HD]
    v = z[:, D + N_KV * HD:D + 2 * N_KV * HD]
    ga = z[:, D + 2 * N_KV * HD:]
    kb = k.astype(bf16)
    vb = v.astype(bf16)

    lower = _lane_lt_half((T_ATT, LANES))
    qm = [_q_for_head(zq, hh, lower) for hh in range(N_HEADS)]
    lower_b = _lane_lt_half((WINDOW, LANES))
    first = (t == 0).astype(jnp.int32)

    nblk = T_ATT // WINDOW
    o_rows = []
    for j in range(nblk):
        r0, r1 = j * WINDOW, (j + 1) * WINDOW
        if j == 0:
            kcat = jnp.concatenate([kprev_ref[...].astype(bf16), kb[r0:r1]], axis=0)
            vcat = jnp.concatenate([vprev_ref[...].astype(bf16), vb[r0:r1]], axis=0)
        else:
            kcat = kb[r0 - WINDOW:r1]
            vcat = vb[r0 - WINDOW:r1]
        o_heads = [None] * N_HEADS
        for p in range(2):
            kp = kcat[:, p * LANES:(p + 1) * LANES]
            vp = vcat[:, p * LANES:(p + 1) * LANES]
            heads = range(8 * p, 8 * p + 8)
            qs = jnp.concatenate([qm[hh][r0:r1] for hh in heads], axis=0)
            s_all = _dot_nt(qs, kp)
            probs, ls = [], []
            for i, hh in enumerate(heads):
                bias = bias_ref[first, hh] if j == 0 else bias_ref[0, hh]
                s = s_all[i * WINDOW:(i + 1) * WINDOW] + bias
                pr, l = _softmax_parts(s, sinks_ref[layer, hh])
                probs.append(pr.astype(bf16))
                ls.append(l)
            o_all = _dot(jnp.concatenate(probs, axis=0), vp)
            for i, hh in enumerate(heads):
                o_heads[hh] = o_all[i * WINDOW:(i + 1) * WINDOW] / ls[i]
        o_rows.append(_heads_to_natural(o_heads, lower_b))
    o_attn = jnp.concatenate(o_rows, axis=0)
    a_ref[0] = _sigmoid(ga) * o_attn

    kprev_ref[...] = k[T_ATT - WINDOW:]
    vprev_ref[...] = v[T_ATT - WINDOW:]

    @pl.when(t == nt - 1)
    def _():
        kw_ref[0] = k[T_ATT - WINDOW:]
        vw_ref[0] = v[T_ATT - WINDOW:]


def _attn_prompt_call(x, mod, mod_row0, nw, w_attn, sinks, bias, layer):
    b, l, _ = x.shape
    wcols = w_attn.shape[-1]
    kvw = N_KV * HD
    return pl.pallas_call(
        functools.partial(_attn_prompt_kernel, layer=layer),
        out_shape=(jax.ShapeDtypeStruct((b, l, D), f32),
                   jax.ShapeDtypeStruct((b, WINDOW, kvw), f32),
                   jax.ShapeDtypeStruct((b, WINDOW, kvw), f32)),
        grid=(b, l // T_ATT),
        in_specs=[
            pl.BlockSpec(memory_space=pltpu.SMEM),
            pl.BlockSpec((1, T_ATT, D), lambda i, t: (i, t, 0)),
            pl.BlockSpec((1, 1, 1, 6 * D), lambda i, t: (layer, mod_row0 + i, 0, 0)),
            _resident((1, 1, D), lambda i, t: (layer, 0, 0)),
            _resident((1, D, wcols), lambda i, t: (layer, 0, 0)),
            _resident(bias.shape, lambda i, t: (0, 0, 0, 0)),
        ],
        out_specs=(pl.BlockSpec((1, T_ATT, D), lambda i, t: (i, t, 0)),
                   pl.BlockSpec((1, WINDOW, kvw), lambda i, t: (i, 0, 0)),
                   pl.BlockSpec((1, WINDOW, kvw), lambda i, t: (i, 0, 0))),
        scratch_shapes=[pltpu.VMEM((WINDOW, kvw), f32),
                        pltpu.VMEM((WINDOW, kvw), f32)],
        compiler_params=_cparams(2),
        name="attn_prompt",
    )(sinks, x, mod, nw, w_attn, bias)


def _attn_sample_kernel(sinks_ref, x_ref, mod_ref, nw_ref, w_ref, ck_ref, cv_ref,
                        bc_ref, bn_ref, a_ref, nk_ref, nv_ref, *, layer):
    sb, r, _ = x_ref.shape
    rows = sb * r
    x = x_ref[...]
    sh = mod_ref[0, :, :, 0:D]
    sc = mod_ref[0, :, :, D:2 * D]
    h = _norm_mod(x, nw_ref[0], sc, sh).reshape(rows, D).astype(bf16)
    z = _dot(h, w_ref[0])
    zq = z[:, 0:D] * (HD ** -0.5)
    kvw = N_KV * HD
    k = z[:, D:D + kvw]
    v = z[:, D + kvw:D + 2 * kvw]
    ga = z[:, D + 2 * kvw:]

    ck = ck_ref[...]
    cv = cv_ref[...]
    w = ck.shape[1]
    nk_ref[:, 0:w - r, :] = ck[:, r:, :]
    nk_ref[:, w - r:, :] = k.reshape(sb, r, kvw)
    nv_ref[:, 0:w - r, :] = cv[:, r:, :]
    nv_ref[:, w - r:, :] = v.reshape(sb, r, kvw)

    ckb = ck.astype(bf16)
    cvb = cv.astype(bf16)
    kb = k.astype(bf16)
    vb = v.astype(bf16)

    lower = _lane_lt_half((rows, LANES))
    o_heads = [None] * N_HEADS
    for p in range(2):
        heads = range(8 * p, 8 * p + 8)
        qs = jnp.concatenate(
            [_q_for_head(zq, hh, lower).reshape(sb, r, LANES) for hh in heads], axis=1)
        kc = ckb[:, :, p * LANES:(p + 1) * LANES]
        vc = cvb[:, :, p * LANES:(p + 1) * LANES]
        kn = kb[:, p * LANES:(p + 1) * LANES]
        vn = vb[:, p * LANES:(p + 1) * LANES]
        s_c = jnp.einsum('bqd,bkd->bqk', qs, kc, preferred_element_type=f32)
        s_n = _dot_nt(qs.reshape(sb * 8 * r, LANES), kn).reshape(sb, 8 * r, rows)
        pc_parts, pn_parts, ls = [], [], []
        for i, hh in enumerate(heads):
            sl = slice(i * r, (i + 1) * r)
            sink = sinks_ref[layer, hh]
            sc_h = s_c[:, sl, :] + bc_ref[hh]
            sn_h = s_n[:, sl, :] + bn_ref[hh]
            m = jnp.maximum(jnp.maximum(jnp.max(sc_h, axis=-1, keepdims=True),
                                        jnp.max(sn_h, axis=-1, keepdims=True)), sink)
            pc = jnp.exp(sc_h - m)
            pn = jnp.exp(sn_h - m)
            l = (jnp.sum(pc, axis=-1, keepdims=True) + jnp.sum(pn, axis=-1, keepdims=True)
                 + jnp.exp(sink - m))
            pc_parts.append(pc.astype(bf16))
            pn_parts.append(pn.astype(bf16))
            ls.append(l)
        pc_all = jnp.concatenate(pc_parts, axis=1)
        pn_all = jnp.concatenate(pn_parts, axis=1)
        o_all = jnp.einsum('bqk,bkd->bqd', pc_all, vc, preferred_element_type=f32)
        o_all = o_all + _dot(pn_all.reshape(sb * 8 * r, rows), vn).reshape(sb, 8 * r, LANES)
        for i, hh in enumerate(heads):
            sl = slice(i * r, (i + 1) * r)
            o_heads[hh] = (o_all[:, sl, :] / ls[i]).reshape(rows, LANES)
    o_attn = _heads_to_natural(o_heads, lower)
    a_ref[...] = (_sigmoid(ga) * o_attn).reshape(sb, r, D)


def _attn_sample_call(x, mod, nw, w_attn, sinks, ck, cv, bias_c, bias_n, layer):
    nb, r, _ = x.shape
    wcols = w_attn.shape[-1]
    kvw = N_KV * HD
    w = ck.shape[2]
    blk3 = lambda i: (i, 0, 0)
    return pl.pallas_call(
        functools.partial(_attn_sample_kernel, layer=layer),
        out_shape=(jax.ShapeDtypeStruct((nb, r, D), f32),
                   jax.ShapeDtypeStruct((nb, w, kvw), f32),
                   jax.ShapeDtypeStruct((nb, w, kvw), f32)),
        grid=(nb // SB_ATT,),
        in_specs=[
            pl.BlockSpec(memory_space=pltpu.SMEM),
            pl.BlockSpec((SB_ATT, r, D), blk3),
            pl.BlockSpec((1, SB_ATT, 1, 6 * D), lambda i: (layer, i, 0, 0)),
            _resident((1, 1, D), lambda i: (layer, 0, 0)),
            _resident((1, D, wcols), lambda i: (layer, 0, 0)),
            pl.BlockSpec((None, SB_ATT, w, kvw), lambda i: (layer, i, 0, 0)),
            pl.BlockSpec((None, SB_ATT, w, kvw), lambda i: (layer, i, 0, 0)),
            _resident(bias_c.shape, lambda i: (0, 0, 0)),
            _resident(bias_n.shape, lambda i: (0, 0, 0, 0)),
        ],
        out_specs=(pl.BlockSpec((SB_ATT, r, D), blk3),
                   pl.BlockSpec((SB_ATT, w, kvw), blk3),
                   pl.BlockSpec((SB_ATT, w, kvw), blk3)),
        compiler_params=_cparams(1),
        name="attn_sample",
    )(sinks, x, mod, nw, w_attn, ck, cv, bias_c, bias_n)


def _hgrn_gates(rq, rf, lb):
    sig = _sigmoid(rf)
    g = jnp.log(lb + (1.0 - lb) * sig)
    kr = (1.0 - lb) * (1.0 - sig)
    qr = _silu(rq) * (DK ** -0.5)
    return qr, kr, g


def _tile_prefix(g3, sub):
    s = g3
    for shift in (1, 2, 4):
        s = s + jnp.where(sub >= shift, pltpu.roll(s, shift, 1), 0.0)
    return s


def _chunk_prefix(g):
    r, c = g.shape
    n = r // SUBLANES
    sub = lax.broadcasted_iota(jnp.int32, (1, SUBLANES, c), 1)
    p8 = _tile_prefix(g.reshape(n, SUBLANES, c), sub)
    tiles = [p8[0]]
    run = p8[0, SUBLANES - 1:SUBLANES, :]
    for i in range(1, n):
        ti = p8[i] + run
        tiles.append(ti)
        run = ti[SUBLANES - 1:SUBLANES, :]
    return jnp.concatenate(tiles, axis=0)


def _level_operand(qr, kr, b, hsz):
    r, c = b.shape
    if hsz >= SUBLANES:
        nb = r // (2 * hsz)
        b4 = b.reshape(nb, 2 * hsz, c)
        d = (b4 - b4[:, hsz - 1:hsz, :]).reshape(r, c)
        parts = []
        for i in range(r // hsz):
            src = qr if i % 2 else kr
            parts.append(src[i * hsz:(i + 1) * hsz])
        src = jnp.concatenate(parts, axis=0)
    else:
        n = r // SUBLANES
        b3 = b.reshape(n, SUBLANES, c)
        sub = lax.broadcasted_iota(jnp.int32, (1, SUBLANES, c), 1)
        if hsz == 4:
            bm = b3[:, 3:4, :]
        elif hsz == 2:
            bm = jnp.where(sub < 4, b3[:, 1:2, :], b3[:, 5:6, :])
        else:
            bm = jnp.where((sub & 1) == 0, b3, pltpu.roll(b3, 1, 1))
        d = (b3 - bm).reshape(r, c)
        upper = (sub & hsz) != 0
        src = jnp.where(upper, qr.reshape(n, SUBLANES, c), kr.reshape(n, SUBLANES, c))
        src = src.reshape(r, c)
    return (src * jnp.exp(jnp.minimum(d, -d))).astype(bf16)


def _intra_scores(qr_b, kr_b, levels, masks_ref, col):
    att = masks_ref[0] * _dot_nt(qr_b[:, col], kr_b[:, col])
    for i, w in enumerate(levels):
        wh = w[:, col]
        att = att + masks_ref[i + 1] * _dot_nt(wh, wh)
    return att


def _head_norm_gate(o, nw, rg):
    ms = jnp.mean(o * o, axis=-1, keepdims=True)
    return o * lax.rsqrt(ms + EPS) * nw * _silu(rg)


def _decay_column(b_last_tile, col_idx):
    tr = b_last_tile.T
    return jnp.broadcast_to(jnp.exp(tr[:, col_idx:col_idx + 1]), (DK, DK))


def _rec_prompt_kernel(x_ref, mod_ref, nw_ref, w_ref, lb_ref, hn_ref, a_ref, wo_ref,
                       masks_ref, o_ref, st_ref, z_ref, mix_ref, s_ref):
    t = pl.program_id(1)
    nt = pl.num_programs(1)

    @pl.when(t == 0)
    def _():
        s_ref[...] = jnp.zeros_like(s_ref)

    x = x_ref[...]
    sh = mod_ref[0, :, :, 0:D]
    sc = mod_ref[0, :, :, D:2 * D]
    ga = mod_ref[0, :, :, 2 * D:3 * D]
    h = _norm_mod(x, nw_ref[0], sc, sh).reshape(T_REC, D).astype(bf16)
    z_ref[...] = _dot(h, w_ref[0])
    lb = lb_ref[0]
    hn = hn_ref[0]
    hs = (1, 2, 4, 8, 16, 32, 64)

    def chunk(c, carry):
        r0 = pl.multiple_of(c * CHUNK, CHUNK)
        rows = pl.ds(r0, CHUNK)
        qr, kr, g = _hgrn_gates(z_ref[rows, 0:D], z_ref[rows, D:2 * D], lb)
        vr = z_ref[rows, 2 * D:3 * D]
        rg = z_ref[rows, 3 * D:4 * D]
        gr = z_ref[rows, 4 * D:5 * D]
        b = _chunk_prefix(g)
        b_last = b[CHUNK - 1:CHUNK, :]
        qe = (qr * jnp.exp(b)).astype(bf16)
        khat = (kr * jnp.exp(b_last - b)).astype(bf16)
        levels = [_level_operand(qr, kr, b, hsz) for hsz in hs]
        qr_b = qr.astype(bf16)
        kr_b = kr.astype(bf16)
        vb = vr.astype(bf16)
        b_tail = b[CHUNK - SUBLANES:, :]
        outs = []
        for hd in range(REC_HEADS):
            col = slice(hd * DK, (hd + 1) * DK)
            s_old = s_ref[hd]
            att = _intra_scores(qr_b, kr_b, levels, masks_ref, col)
            o = _dot(qe[:, col], s_old.astype(bf16)) + _dot(att.astype(bf16), vb[:, col])
            dcol = _decay_column(b_tail[:, col], SUBLANES - 1)
            s_ref[hd] = dcol * s_old + _dot_tn(khat[:, col], vb[:, col])
            outs.append(_head_norm_gate(o, hn[:, col], rg[:, col]))
        o_rec = jnp.concatenate(outs, axis=1)
        mixed = a_ref[0, rows, :] + _sigmoid(gr) * o_rec
        mix_ref[rows, :] = mixed.astype(bf16)
        return carry

    lax.fori_loop(0, T_REC // CHUNK, chunk, 0)
    y = _dot(mix_ref[...], wo_ref[0])
    o_ref[...] = x + ga * y.reshape(1, T_REC, D)

    @pl.when(t == nt - 1)
    def _():
        st_ref[0] = s_ref[...]


def _rec_prompt_call(x, mod, mod_row0, nw, w_rec, lb, hn, a_mix, w_out, masks, layer):
    b, l, _ = x.shape
    wcols = w_rec.shape[-1]
    return pl.pallas_call(
        _rec_prompt_kernel,
        out_shape=(jax.ShapeDtypeStruct((b, l, D), f32),
                   jax.ShapeDtypeStruct((b, REC_HEADS, DK, DK), f32)),
        grid=(b, l // T_REC),
        in_specs=[
            pl.BlockSpec((1, T_REC, D), lambda i, t: (i, t, 0)),
            pl.BlockSpec((1, 1, 1, 6 * D), lambda i, t: (layer, mod_row0 + i, 0, 0)),
            _resident((1, 1, D), lambda i, t: (layer, 0, 0)),
            _resident((1, D, wcols), lambda i, t: (layer, 0, 0)),
            _resident((1, 1, D), lambda i, t: (layer, 0, 0)),
            _resident((1, 1, D), lambda i, t: (layer, 0, 0)),
            pl.BlockSpec((1, T_REC, D), lambda i, t: (i, t, 0)),
            _resident((1, D, D), lambda i, t: (layer, 0, 0)),
            _resident(masks.shape, lambda i, t: (0, 0, 0)),
        ],
        out_specs=(pl.BlockSpec((1, T_REC, D), lambda i, t: (i, t, 0)),
                   pl.BlockSpec((1, REC_HEADS, DK, DK), lambda i, t: (i, 0, 0, 0))),
        scratch_shapes=[pltpu.VMEM((T_REC, wcols), f32),
                        pltpu.VMEM((T_REC, D), bf16),
                        pltpu.VMEM((REC_HEADS, DK, DK), f32)],
        compiler_params=_cparams(2),
        name="rec_prompt",
    )(x, mod, nw, w_rec, lb, hn, a_mix, w_out, masks)


def _rec_sample_kernel(x_ref, mod_ref, nw_ref, w_ref, lb_ref, hn_ref, a_ref, wo_ref,
                       masks_ref, s0_ref, o_ref, s1_ref):
    sb, r, _ = x_ref.shape
    rows = sb * r
    x = x_ref[...]
    sh = mod_ref[0, :, :, 0:D]
    sc = mod_ref[0, :, :, D:2 * D]
    ga = mod_ref[0, :, :, 2 * D:3 * D]
    h = _norm_mod(x, nw_ref[0], sc, sh).reshape(rows, D).astype(bf16)
    z = _dot(h, w_ref[0])
    qr, kr, g = _hgrn_gates(z[:, 0:D], z[:, D:2 * D], lb_ref[0])
    vr = z[:, 2 * D:3 * D]
    rg = z[:, 3 * D:4 * D]
    gr = z[:, 4 * D:5 * D]
    hn = hn_ref[0]

    sub = lax.broadcasted_iota(jnp.int32, (1, SUBLANES, D), 1)
    b3 = _tile_prefix(g.reshape(sb, r, D), sub)
    b = b3.reshape(rows, D)
    b_last3 = b3[:, r - 1:r, :]
    qe3 = (qr.reshape(sb, r, D) * jnp.exp(b3)).astype(bf16)
    khat3 = (kr.reshape(sb, r, D) * jnp.exp(b_last3 - b3)).astype(bf16)
    levels = [_level_operand(qr, kr, b, hsz) for hsz in (1, 2, 4)]
    qr_b = qr.astype(bf16)
    kr_b = kr.astype(bf16)
    vb = vr.astype(bf16)
    vb3 = vb.reshape(sb, r, D)
    b_last = b_last3.reshape(sb, D)

    outs = []
    for hd in range(REC_HEADS):
        col = slice(hd * DK, (hd + 1) * DK)
        s_old = s0_ref[:, hd]
        att = _intra_scores(qr_b, kr_b, levels, masks_ref, col)
        o_intra = _dot(att.astype(bf16), vb[:, col])
        o_inter = jnp.einsum('bqk,bkv->bqv', qe3[:, :, col], s_old.astype(bf16),
                             preferred_element_type=f32)
        upd = jnp.einsum('btk,btv->bkv', khat3[:, :, col], vb3[:, :, col],
                         preferred_element_type=f32)
        bl = b_last[:, col]
        for i in range(sb):
            s1_ref[i, hd] = _decay_column(bl, i) * s_old[i] + upd[i]
        o = o_inter.reshape(rows, DK) + o_intra
        outs.append(_head_norm_gate(o, hn[:, col], rg[:, col]))
    o_rec = jnp.concatenate(outs, axis=1)
    mixed = a_ref[...].reshape(rows, D) + _sigmoid(gr) * o_rec
    y = _dot(mixed.astype(bf16), wo_ref[0])
    o_ref[...] = x + ga * y.reshape(sb, r, D)


def _rec_sample_call(x, mod, nw, w_rec, lb, hn, a_mix, w_out, masks, s0, layer):
    nb, r, _ = x.shape
    wcols = w_rec.shape[-1]
    blk3 = lambda i: (i, 0, 0)
    return pl.pallas_call(
        _rec_sample_kernel,
        out_shape=(jax.ShapeDtypeStruct((nb, r, D), f32),
                   jax.ShapeDtypeStruct((nb, REC_HEADS, DK, DK), f32)),
        grid=(nb // SB_REC,),
        in_specs=[
            pl.BlockSpec((SB_REC, r, D), blk3),
            pl.BlockSpec((1, SB_REC, 1, 6 * D), lambda i: (layer, i, 0, 0)),
            _resident((1, 1, D), lambda i: (layer, 0, 0)),
            _resident((1, D, wcols), lambda i: (layer, 0, 0)),
            _resident((1, 1, D), lambda i: (layer, 0, 0)),
            _resident((1, 1, D), lambda i: (layer, 0, 0)),
            pl.BlockSpec((SB_REC, r, D), blk3),
            _resident((1, D, D), lambda i: (layer, 0, 0)),
            _resident(masks.shape, lambda i: (0, 0, 0)),
            pl.BlockSpec((None, SB_REC, REC_HEADS, DK, DK), lambda i: (layer, i, 0, 0, 0)),
        ],
        out_specs=(pl.BlockSpec((SB_REC, r, D), blk3),
                   pl.BlockSpec((SB_REC, REC_HEADS, DK, DK), lambda i: (i, 0, 0, 0))),
        compiler_params=_cparams(1),
        name="rec_sample",
    )(x, mod, nw, w_rec, lb, hn, a_mix, w_out, masks, s0)


def _alibi_slopes():
    return 2.0 ** (-8.0 * jnp.arange(1, N_HEADS + 1, dtype=f32) / N_HEADS)


def _prompt_bias():
    t = np.arange(WINDOW)[:, None]
    c = np.arange(2 * WINDOW)[None, :]
    dist = WINDOW + t - c
    valid = (dist >= 0) & (dist <= WINDOW)
    valid = np.stack([valid, valid & (c >= WINDOW)])
    slopes = _alibi_slopes()[None, :, None, None]
    return jnp.where(jnp.asarray(valid)[:, None], -slopes * jnp.asarray(dist, f32), MASK_VALUE)


def _sample_bias(r, w, sb):
    t = np.arange(r)[:, None]
    c = np.arange(w)[None, :]
    dist_c = w + t - c
    valid_c = (dist_c >= 0) & (dist_c <= WINDOW)
    slopes = _alibi_slopes()
    bias_c = jnp.where(jnp.asarray(valid_c)[None], -slopes[:, None, None] * jnp.asarray(dist_c, f32),
                       MASK_VALUE)
    bq = np.arange(sb)[:, None, None, None]
    tq = np.arange(r)[None, :, None, None]
    bk = np.arange(sb)[None, None, :, None]
    tk = np.arange(r)[None, None, None, :]
    dist_n = np.broadcast_to(tq - tk, (sb, r, sb, r)).reshape(sb, r, sb * r)
    valid_n = np.broadcast_to((bq == bk) & (tq >= tk), (sb, r, sb, r)).reshape(sb, r, sb * r)
    bias_n = jnp.where(jnp.asarray(valid_n)[None],
                       -slopes[:, None, None, None] * jnp.asarray(dist_n, f32), MASK_VALUE)
    return bias_c, bias_n


def _level_masks(rows, hs):
    t = np.arange(rows)[:, None]
    s = np.arange(rows)[None, :]
    out = [t == s]
    for hsz in hs:
        same = (t // (2 * hsz)) == (s // (2 * hsz))
        out.append(same & ((t % (2 * hsz)) >= hsz) & ((s % (2 * hsz)) < hsz))
    return jnp.asarray(np.stack(out), f32)


def kernel(x_prompt, x_sample, cache_win_k, cache_win_v, state_hgrn, c_prompt, c_sample,
           norm_mix, norm_mlp, w_ada, b_ada, w_in, attn_sinks, hgrn_lb_logits, hgrn_norm,
           w_out, w_up, w_down, final_norm):
    depth = w_in.shape[0]
    nb_p, seq, _ = x_prompt.shape
    nb_s, dec, _ = x_sample.shape
    w_buf = cache_win_k.shape[2]
    kvw = N_KV * HD
    assert seq % T_ATT == 0 and seq % T_REC == 0 and T_MLP == T_REC
    assert dec == SUBLANES and w_buf == WINDOW
    assert nb_s % SB_ATT == 0 and nb_s % SB_REC == 0 and nb_s % SB_MLP == 0

    pad = (-(nb_s + nb_p)) % SUBLANES
    c_all = jnp.concatenate([c_sample, c_prompt, jnp.zeros((pad, D), f32)], axis=0)
    n_c = c_all.shape[0]
    n_mod = 6 * D
    mod_cols = 1536
    mod = pl.pallas_call(
        _mod_kernel,
        out_shape=jax.ShapeDtypeStruct((depth, n_c, n_mod), f32),
        grid=(depth, n_mod // mod_cols),
        in_specs=[
            _resident((n_c, D), lambda l, j: (0, 0)),
            pl.BlockSpec((1, D, mod_cols), lambda l, j: (l, 0, j)),
            pl.BlockSpec((1, 1, mod_cols), lambda l, j: (l, 0, j)),
        ],
        out_specs=pl.BlockSpec((1, n_c, mod_cols), lambda l, j: (l, 0, j)),
        compiler_params=_cparams(2),
        name="adaln_mod",
    )(c_all, w_ada, b_ada.reshape(depth, 1, n_mod))
    mod = mod.reshape(depth, n_c, 1, n_mod)
    prompt_row0 = nb_s

    lb_all = pl.pallas_call(
        _lb_kernel,
        out_shape=jax.ShapeDtypeStruct(hgrn_lb_logits.shape, f32),
        name="hgrn_lower_bound",
    )(hgrn_lb_logits).reshape(depth, 1, D)

    q_end = D
    v_end = D + 2 * kvw
    rec_end = v_end + 4 * D
    w_attn = jnp.concatenate([w_in[:, :, :v_end], w_in[:, :, rec_end:rec_end + D]],
                             axis=-1).astype(bf16)
    w_rec = jnp.concatenate([w_in[:, :, v_end:rec_end], w_in[:, :, rec_end + D:]],
                            axis=-1).astype(bf16)
    del q_end
    w_out_b = w_out.astype(bf16)
    w_up_b = w_up.astype(bf16)
    w_dn_b = w_down.astype(bf16)

    nmix = norm_mix.reshape(depth, 1, D)
    nmlp = norm_mlp.reshape(depth, 1, D)
    hnorm = hgrn_norm.reshape(depth, 1, D)
    fnorm = final_norm.reshape(1, D)

    bias_p = _prompt_bias()
    bias_c, bias_n = _sample_bias(dec, w_buf, SB_ATT)
    masks_p = _level_masks(CHUNK, (1, 2, 4, 8, 16, 32, 64))
    masks_s = _level_masks(SB_REC * dec, (1, 2, 4))

    ck_all = cache_win_k.reshape(depth, nb_s, w_buf, kvw)
    cv_all = cache_win_v.reshape(depth, nb_s, w_buf, kvw)

    xp, xs = x_prompt, x_sample
    pk, pv, ps, sk, sv, ss = [], [], [], [], [], []
    for l in range(depth):
        final = l == depth - 1
        a_p, k_p, v_p = _attn_prompt_call(xp, mod, prompt_row0, nmix, w_attn, attn_sinks,
                                          bias_p, l)
        xp, s_p = _rec_prompt_call(xp, mod, prompt_row0, nmix, w_rec, lb_all, hnorm, a_p,
                                   w_out_b, masks_p, l)
        xp = _mlp_prompt(xp, mod, prompt_row0, nmlp, w_up_b, w_dn_b, fnorm, l, final)

        a_s, k_s, v_s = _attn_sample_call(xs, mod, nmix, w_attn, attn_sinks, ck_all, cv_all,
                                          bias_c, bias_n, l)
        xs, s_s = _rec_sample_call(xs, mod, nmix, w_rec, lb_all, hnorm, a_s, w_out_b,
                                   masks_s, state_hgrn, l)
        xs = _mlp_call(xs, mod, 0, nmlp, w_up_b, w_dn_b, fnorm, l, final, SB_MLP)

        pk.append(k_p.reshape(nb_p, w_buf, N_KV, HD))
        pv.append(v_p.reshape(nb_p, w_buf, N_KV, HD))
        ps.append(s_p)
        sk.append(k_s.reshape(nb_s, w_buf, N_KV, HD))
        sv.append(v_s.reshape(nb_s, w_buf, N_KV, HD))
        ss.append(s_s)

    return (xp, xs, jnp.stack(pk), jnp.stack(pv), jnp.stack(ps),
            jnp.stack(sk), jnp.stack(sv), jnp.stack(ss))


def _mlp_prompt(x, mod, mod_row0, nw, wup, wdn, fn, layer, final):
    b, l, _ = x.shape
    return pl.pallas_call(
        functools.partial(_mlp_kernel, final=final),
        out_shape=jax.ShapeDtypeStruct(x.shape, f32),
        grid=(b, l // T_MLP),
        in_specs=[
            pl.BlockSpec((1, T_MLP, D), lambda i, t: (i, t, 0)),
            pl.BlockSpec((1, 1, 1, 6 * D), lambda i, t: (layer, mod_row0 + i, 0, 0)),
            _resident((1, 1, D), lambda i, t: (layer, 0, 0)),
            _resident((1, D, D_FF), lambda i, t: (layer, 0, 0)),
            _resident((1, D_FF, D), lambda i, t: (layer, 0, 0)),
            _resident((1, D), lambda i, t: (0, 0)),
        ],
        out_specs=pl.BlockSpec((1, T_MLP, D), lambda i, t: (i, t, 0)),
        compiler_params=_cparams(2),
        name="mlp_prompt",
    )(x, mod, nw, wup, wdn, fn)
```

```python
import functools

import numpy as np
import jax
import jax.numpy as jnp
from jax import lax
from jax.experimental import pallas as pl
from jax.experimental.pallas import tpu as pltpu

f32 = jnp.float32
bf16 = jnp.bfloat16

D = 1024
N_HEADS = 16
N_KV = 4
HD = 64
WINDOW = 128
REC_HEADS = 8
DK = 128
D_FF = 4 * D
EPS = 1e-6
MASK_VALUE = -1e30
LANES = 128
SUBLANES = 8
VMEM_LIMIT = 56 * 1024 * 1024

T_ATT = 512
T_REC = 1024
T_MLP = 512
CHUNK = 128
SB_ATT = 16
SB_REC = 8
SB_MLP = 64


def _cparams(n_axes):
    return pltpu.CompilerParams(
        dimension_semantics=("arbitrary",) * n_axes, vmem_limit_bytes=VMEM_LIMIT)


def _resident(shape, index_map):
    return pl.BlockSpec(shape, index_map, pipeline_mode=pl.Buffered(1))


def _dot(a, b):
    return jnp.dot(a, b, preferred_element_type=f32)


def _pack_rows(w):
    *lead, k, n = w.shape
    w2 = jnp.swapaxes(w.astype(bf16).reshape(*lead, k // 2, 2, n), -1, -2)
    return lax.bitcast_convert_type(w2, jnp.uint32)


def _unpack_rows(w):
    return pltpu.bitcast(w, bf16)


def _dot_nt(a, b):
    return lax.dot_general(a, b, (((1,), (1,)), ((), ())), preferred_element_type=f32)


def _dot_tn(a, b):
    return lax.dot_general(a, b, (((0,), (0,)), ((), ())), preferred_element_type=f32)


def _sigmoid(x):
    return 0.5 + 0.5 * jnp.tanh(0.5 * x)


def _silu(x):
    hx = 0.5 * x
    return hx + hx * jnp.tanh(hx)


def _norm_mod(x, nw, sc, sh):
    ms = jnp.mean(x * x, axis=-1, keepdims=True)
    y = x * lax.rsqrt(ms + EPS) * nw
    return y * (1.0 + sc) + sh


def _mod_kernel(c_ref, w_ref, b_ref, o_ref):
    c = c_ref[...]
    s = _silu(c).astype(bf16)
    o_ref[0] = _dot(s, w_ref[0].astype(bf16)) + b_ref[0]


def _lb_kernel(l_ref, o_ref):
    rows = [l_ref[i:i + 1, :] for i in range(l_ref.shape[0])]
    m = functools.reduce(jnp.maximum, rows)
    e = [jnp.exp(r - m) for r in rows]
    tot = functools.reduce(lambda a, b: a + b, e)
    p = [x / tot for x in e]
    run = p[0]
    o_ref[0:1, :] = run - p[0]
    for i in range(1, len(p)):
        run = run + p[i]
        o_ref[i:i + 1, :] = run - p[0]


def _mlp_kernel(x_ref, mod_ref, nw_ref, wup_ref, wdn_ref, fn_ref, o_ref, *, final):
    x = x_ref[...]
    g, r, _ = x.shape
    sh = mod_ref[0, :, :, 3 * D:4 * D]
    sc = mod_ref[0, :, :, 4 * D:5 * D]
    gm = mod_ref[0, :, :, 5 * D:6 * D]
    h = _norm_mod(x, nw_ref[0], sc, sh).reshape(g * r, D).astype(bf16)
    acc = jnp.zeros((g * r, D), f32)
    for c in range(D_FF // D):
        u = _dot(h, _unpack_rows(wup_ref[0, :, c * D:(c + 1) * D]))
        u = jnp.square(jnp.maximum(u, 0.0)).astype(bf16)
        acc = acc + _dot(u, _unpack_rows(wdn_ref[0, c * D // 2:(c + 1) * D // 2, :]))
    y = x + gm * acc.reshape(g, r, D)
    if final:
        ms = jnp.mean(y * y, axis=-1, keepdims=True)
        y = y * lax.rsqrt(ms + EPS) * fn_ref[...]
    o_ref[...] = y


def _mlp_call(x, mod, mod_row0, nw, wup, wdn, fn, layer, final, gblk):
    ng, r, _ = x.shape
    grid = (ng // gblk,)
    mrow = mod_row0 // gblk
    return pl.pallas_call(
        functools.partial(_mlp_kernel, final=final),
        out_shape=jax.ShapeDtypeStruct(x.shape, f32),
        grid=grid,
        in_specs=[
            pl.BlockSpec((gblk, r, D), lambda i: (i, 0, 0)),
            pl.BlockSpec((1, gblk, 1, 6 * D), lambda i: (layer, mrow + i, 0, 0)),
            _resident((1, 1, D), lambda i: (layer, 0, 0)),
            _resident((1, D // 2, D_FF), lambda i: (layer, 0, 0)),
            _resident((1, D_FF // 2, D), lambda i: (layer, 0, 0)),
            _resident((1, D), lambda i: (0, 0)),
        ],
        out_specs=pl.BlockSpec((gblk, r, D), lambda i: (i, 0, 0)),
        compiler_params=_cparams(1),
        name="mlp",
    )(x, mod, nw, wup, wdn, fn)


def _lane_lt_half(shape):
    return lax.broadcasted_iota(jnp.int32, shape, len(shape) - 1) < HD


def _q_for_head(zq, h, lower):
    blk = zq[:, (h // 2) * LANES:(h // 2 + 1) * LANES]
    own, tgt = h % 2, (h // 4) % 2
    if own != tgt:
        blk = pltpu.roll(blk, HD, 1)
    keep = lower if tgt == 0 else jnp.logical_not(lower)
    return jnp.where(keep, blk, 0.0).astype(bf16)


def _heads_to_natural(o_heads, lower):
    cols = []
    for jn in range(N_HEADS // 2):
        tgt = (jn // 2) % 2
        a, b = o_heads[2 * jn], o_heads[2 * jn + 1]
        if tgt == 1:
            a = pltpu.roll(a, HD, 1)
        else:
            b = pltpu.roll(b, HD, 1)
        cols.append(jnp.where(lower, a, b))
    return jnp.concatenate(cols, axis=1)


def _softmax_parts(s, sink):
    m = jnp.maximum(jnp.max(s, axis=-1, keepdims=True), sink)
    p = jnp.exp(s - m)
    l = jnp.sum(p, axis=-1, keepdims=True) + jnp.exp(sink - m)
    return p, l


def _attn_prompt_kernel(sinks_ref, x_ref, mod_ref, nw_ref, w_ref, bias_ref,
                        a_ref, kw_ref, vw_ref, kprev_ref, vprev_ref, *, layer):
    t = pl.program_id(1)
    nt = pl.num_programs(1)

    @pl.when(t == 0)
    def _():
        kprev_ref[...] = jnp.zeros_like(kprev_ref)
        vprev_ref[...] = jnp.zeros_like(vprev_ref)

    x = x_ref[...]
    sh = mod_ref[0, :, :, 0:D]
    sc = mod_ref[0, :, :, D:2 * D]
    h = _norm_mod(x, nw_ref[0], sc, sh).reshape(T_ATT, D).astype(bf16)
    z = _dot(h, _unpack_rows(w_ref[0]))
    zq = z[:, 0:D] * (HD ** -0.5)
    k = z[:, D:D + N_KV * HD]
    v = z[:, D + N_KV * HD:D + 2 * N_KV * HD]
    ga = z[:, D + 2 * N_KV * HD:]
    kb = k.astype(bf16)
    vb = v.astype(bf16)

    lower = _lane_lt_half((T_ATT, LANES))
    qm = [_q_for_head(zq, hh, lower) for hh in range(N_HEADS)]
    lower_b = _lane_lt_half((WINDOW, LANES))
    first = (t == 0).astype(jnp.int32)

    nblk = T_ATT // WINDOW
    o_rows = []
    for j in range(nblk):
        r0, r1 = j * WINDOW, (j + 1) * WINDOW
        if j == 0:
            kcat = jnp.concatenate([kprev_ref[...].astype(bf16), kb[r0:r1]], axis=0)
            vcat = jnp.concatenate([vprev_ref[...].astype(bf16), vb[r0:r1]], axis=0)
        else:
            kcat = kb[r0 - WINDOW:r1]
            vcat = vb[r0 - WINDOW:r1]
        o_heads = [None] * N_HEADS
        for p in range(2):
            kp = kcat[:, p * LANES:(p + 1) * LANES]
            vp = vcat[:, p * LANES:(p + 1) * LANES]
            heads = range(8 * p, 8 * p + 8)
            qs = jnp.concatenate([qm[hh][r0:r1] for hh in heads], axis=0)
            s_all = _dot_nt(qs, kp)
            probs, ls = [], []
            for i, hh in enumerate(heads):
                bias = bias_ref[first, hh] if j == 0 else bias_ref[0, hh]
                s = s_all[i * WINDOW:(i + 1) * WINDOW] + bias
                pr, l = _softmax_parts(s, sinks_ref[layer, hh])
                probs.append(pr.astype(bf16))
                ls.append(l)
            o_all = _dot(jnp.concatenate(probs, axis=0), vp)
            for i, hh in enumerate(heads):
                o_heads[hh] = o_all[i * WINDOW:(i + 1) * WINDOW] / ls[i]
        o_rows.append(_heads_to_natural(o_heads, lower_b))
    o_attn = jnp.concatenate(o_rows, axis=0)
    a_ref[0] = _sigmoid(ga) * o_attn

    kprev_ref[...] = k[T_ATT - WINDOW:]
    vprev_ref[...] = v[T_ATT - WINDOW:]

    @pl.when(t == nt - 1)
    def _():
        kw_ref[0] = k[T_ATT - WINDOW:]
        vw_ref[0] = v[T_ATT - WINDOW:]


def _attn_prompt_call(x, mod, mod_row0, nw, w_attn, sinks, bias, layer):
    b, l, _ = x.shape
    wcols = w_attn.shape[-1]
    kvw = N_KV * HD
    return pl.pallas_call(
        functools.partial(_attn_prompt_kernel, layer=layer),
        out_shape=(jax.ShapeDtypeStruct((b, l, D), f32),
                   jax.ShapeDtypeStruct((b, WINDOW, kvw), f32),
                   jax.ShapeDtypeStruct((b, WINDOW, kvw), f32)),
        grid=(b, l // T_ATT),
        in_specs=[
            pl.BlockSpec(memory_space=pltpu.SMEM),
            pl.BlockSpec((1, T_ATT, D), lambda i, t: (i, t, 0)),
            pl.BlockSpec((1, 1, 1, 6 * D), lambda i, t: (layer, mod_row0 + i, 0, 0)),
            _resident((1, 1, D), lambda i, t: (layer, 0, 0)),
            _resident((1, D // 2, wcols), lambda i, t: (layer, 0, 0)),
            _resident(bias.shape, lambda i, t: (0, 0, 0, 0)),
        ],
        out_specs=(pl.BlockSpec((1, T_ATT, D), lambda i, t: (i, t, 0)),
                   pl.BlockSpec((1, WINDOW, kvw), lambda i, t: (i, 0, 0)),
                   pl.BlockSpec((1, WINDOW, kvw), lambda i, t: (i, 0, 0))),
        scratch_shapes=[pltpu.VMEM((WINDOW, kvw), f32),
                        pltpu.VMEM((WINDOW, kvw), f32)],
        compiler_params=_cparams(2),
        name="attn_prompt",
    )(sinks, x, mod, nw, w_attn, bias)


def _attn_sample_kernel(sinks_ref, x_ref, mod_ref, nw_ref, w_ref, ck_ref, cv_ref,
                        bc_ref, bn_ref, a_ref, nk_ref, nv_ref, *, layer):
    sb, r, _ = x_ref.shape
    rows = sb * r
    x = x_ref[...]
    sh = mod_ref[0, :, :, 0:D]
    sc = mod_ref[0, :, :, D:2 * D]
    h = _norm_mod(x, nw_ref[0], sc, sh).reshape(rows, D).astype(bf16)
    z = _dot(h, _unpack_rows(w_ref[0]))
    zq = z[:, 0:D] * (HD ** -0.5)
    kvw = N_KV * HD
    k = z[:, D:D + kvw]
    v = z[:, D + kvw:D + 2 * kvw]
    ga = z[:, D + 2 * kvw:]

    ck = ck_ref[...]
    cv = cv_ref[...]
    w = ck.shape[1]
    nk_ref[:, 0:w - r, :] = ck[:, r:, :]
    nk_ref[:, w - r:, :] = k.reshape(sb, r, kvw)
    nv_ref[:, 0:w - r, :] = cv[:, r:, :]
    nv_ref[:, w - r:, :] = v.reshape(sb, r, kvw)

    ckb = ck.astype(bf16)
    cvb = cv.astype(bf16)
    kb = k.astype(bf16)
    vb = v.astype(bf16)

    lower = _lane_lt_half((rows, LANES))
    o_heads = [None] * N_HEADS
    for p in range(2):
        heads = range(8 * p, 8 * p + 8)
        qs = jnp.concatenate(
            [_q_for_head(zq, hh, lower).reshape(sb, r, LANES) for hh in heads], axis=1)
        kc = ckb[:, :, p * LANES:(p + 1) * LANES]
        vc = cvb[:, :, p * LANES:(p + 1) * LANES]
        kn = kb[:, p * LANES:(p + 1) * LANES]
        vn = vb[:, p * LANES:(p + 1) * LANES]
        s_c = jnp.einsum('bqd,bkd->bqk', qs, kc, preferred_element_type=f32)
        s_n = _dot_nt(qs.reshape(sb * 8 * r, LANES), kn).reshape(sb, 8 * r, rows)
        pc_parts, pn_parts, ls = [], [], []
        for i, hh in enumerate(heads):
            sl = slice(i * r, (i + 1) * r)
            sink = sinks_ref[layer, hh]
            sc_h = s_c[:, sl, :] + bc_ref[hh]
            sn_h = s_n[:, sl, :] + bn_ref[hh]
            m = jnp.maximum(jnp.maximum(jnp.max(sc_h, axis=-1, keepdims=True),
                                        jnp.max(sn_h, axis=-1, keepdims=True)), sink)
            pc = jnp.exp(sc_h - m)
            pn = jnp.exp(sn_h - m)
            l = (jnp.sum(pc, axis=-1, keepdims=True) + jnp.sum(pn, axis=-1, keepdims=True)
                 + jnp.exp(sink - m))
            pc_parts.append(pc.astype(bf16))
            pn_parts.append(pn.astype(bf16))
            ls.append(l)
        pc_all = jnp.concatenate(pc_parts, axis=1)
        pn_all = jnp.concatenate(pn_parts, axis=1)
        o_all = jnp.einsum('bqk,bkd->bqd', pc_all, vc, preferred_element_type=f32)
        o_all = o_all + _dot(pn_all.reshape(sb * 8 * r, rows), vn).reshape(sb, 8 * r, LANES)
        for i, hh in enumerate(heads):
            sl = slice(i * r, (i + 1) * r)
            o_heads[hh] = (o_all[:, sl, :] / ls[i]).reshape(rows, LANES)
    o_attn = _heads_to_natural(o_heads, lower)
    a_ref[...] = (_sigmoid(ga) * o_attn).reshape(sb, r, D)


def _attn_sample_call(x, mod, nw, w_attn, sinks, ck, cv, bias_c, bias_n, layer):
    nb, r, _ = x.shape
    wcols = w_attn.shape[-1]
    kvw = N_KV * HD
    w = ck.shape[2]
    blk3 = lambda i: (i, 0, 0)
    return pl.pallas_call(
        functools.partial(_attn_sample_kernel, layer=layer),
        out_shape=(jax.ShapeDtypeStruct((nb, r, D), f32),
                   jax.ShapeDtypeStruct((nb, w, kvw), f32),
                   jax.ShapeDtypeStruct((nb, w, kvw), f32)),
        grid=(nb // SB_ATT,),
        in_specs=[
            pl.BlockSpec(memory_space=pltpu.SMEM),
            pl.BlockSpec((SB_ATT, r, D), blk3),
            pl.BlockSpec((1, SB_ATT, 1, 6 * D), lambda i: (layer, i, 0, 0)),
            _resident((1, 1, D), lambda i: (layer, 0, 0)),
            _resident((1, D // 2, wcols), lambda i: (layer, 0, 0)),
            pl.BlockSpec((None, SB_ATT, w, kvw), lambda i: (layer, i, 0, 0)),
            pl.BlockSpec((None, SB_ATT, w, kvw), lambda i: (layer, i, 0, 0)),
            _resident(bias_c.shape, lambda i: (0, 0, 0)),
            _resident(bias_n.shape, lambda i: (0, 0, 0, 0)),
        ],
        out_specs=(pl.BlockSpec((SB_ATT, r, D), blk3),
                   pl.BlockSpec((SB_ATT, w, kvw), blk3),
                   pl.BlockSpec((SB_ATT, w, kvw), blk3)),
        compiler_params=_cparams(1),
        name="attn_sample",
    )(sinks, x, mod, nw, w_attn, ck, cv, bias_c, bias_n)


def _hgrn_gates(rq, rf, lb):
    c1 = 0.5 * (1.0 - lb)
    u = c1 * jnp.tanh(0.5 * rf)
    g2 = jnp.log2((0.5 * (1.0 + lb)) + u)
    kr = c1 - u
    hq = 0.5 * rq
    qr = (hq * (DK ** -0.5)) * (1.0 + jnp.tanh(hq))
    return qr, kr, g2


def _tile_prefix(g3, sub):
    s = g3
    for shift in (1, 2, 4):
        s = s + jnp.where(sub >= shift, pltpu.roll(s, shift, 1), 0.0)
    return s


def _chunk_prefix(g):
    r, c = g.shape
    n = r // SUBLANES
    sub = lax.broadcasted_iota(jnp.int32, (1, SUBLANES, c), 1)
    p8 = _tile_prefix(g.reshape(n, SUBLANES, c), sub)
    tiles = [p8[0]]
    run = p8[0, SUBLANES - 1:SUBLANES, :]
    for i in range(1, n):
        ti = p8[i] + run
        tiles.append(ti)
        run = ti[SUBLANES - 1:SUBLANES, :]
    return jnp.concatenate(tiles, axis=0)


def _level_operands(qr, kr, b, hsz):
    r, c = b.shape
    if hsz >= SUBLANES:
        nb = r // (2 * hsz)
        b4 = b.reshape(nb, 2, hsz, c)
        bm = b4[:, 0, hsz - 1:hsz, :]
        e_lo = jnp.exp2(bm - b4[:, 0])
        e_up = jnp.exp2(b4[:, 1] - bm)
        e = jnp.stack([e_lo, e_up], axis=1).reshape(r, c)
        lhs = (qr.reshape(nb, 2, hsz, c)[:, 1] * e_up).reshape(nb * hsz, c)
        return lhs.astype(bf16), (kr * e).astype(bf16)
    n = r // SUBLANES
    b3 = b.reshape(n, SUBLANES, c)
    sub = lax.broadcasted_iota(jnp.int32, (1, SUBLANES, c), 1)
    if hsz == 4:
        bm = b3[:, 3:4, :]
    elif hsz == 2:
        bm = jnp.where(sub < 4, b3[:, 1:2, :], b3[:, 5:6, :])
    else:
        bm = jnp.where((sub & 1) == 0, b3, pltpu.roll(b3, 1, 1))
    d = (b3 - bm).reshape(r, c)
    upper = (sub & hsz) != 0
    src = jnp.where(upper, qr.reshape(n, SUBLANES, c), kr.reshape(n, SUBLANES, c))
    nd = lax.bitcast_convert_type(
        lax.bitcast_convert_type(d, jnp.uint32) | jnp.uint32(0x80000000), f32)
    w = (src.reshape(r, c) * jnp.exp2(nd)).astype(bf16)
    return w, w


def _intra_scores(qr_b, kr_b, levels, hs, masks_ref, col):
    r = qr_b.shape[0]
    att = masks_ref[0] * _dot_nt(qr_b[:, col], kr_b[:, col])
    tiled = []
    for i, ((lhs, rhs), hsz) in enumerate(zip(levels, hs)):
        if hsz < SUBLANES:
            att = att + masks_ref[i + 1] * _dot_nt(lhs[:, col], rhs[:, col])
        else:
            tiled.append((i + 1, hsz, _dot_nt(lhs[:, col], rhs[:, col])))
    if not tiled:
        return att
    tiles = [att[j * SUBLANES:(j + 1) * SUBLANES] for j in range(r // SUBLANES)]
    for idx, hsz, m in tiled:
        row = 0
        for blk in range(r // (2 * hsz)):
            for j in range(hsz // SUBLANES):
                t0 = blk * 2 * hsz + hsz + j * SUBLANES
                tiles[t0 // SUBLANES] = (tiles[t0 // SUBLANES]
                                         + masks_ref[idx, t0:t0 + SUBLANES, :]
                                         * m[row:row + SUBLANES])
                row += SUBLANES
    return jnp.concatenate(tiles, axis=0)


def _head_norm_gate(o, nw, rg):
    ms = jnp.mean(o * o, axis=-1, keepdims=True)
    return o * lax.rsqrt(ms + EPS) * nw * _silu(rg)


def _decay_column(b_last_tile, col_idx):
    tr = b_last_tile.T
    return jnp.broadcast_to(jnp.exp2(tr[:, col_idx:col_idx + 1]), (DK, DK))


def _rec_prompt_kernel(x_ref, mod_ref, nw_ref, w_ref, lb_ref, hn_ref, a_ref, wo_ref,
                       masks_ref, o_ref, st_ref, h_ref, z0_ref, z1_ref, s_ref):
    t = pl.program_id(1)
    nt = pl.num_programs(1)

    @pl.when(t == 0)
    def _():
        s_ref[...] = jnp.zeros_like(s_ref)

    sh = mod_ref[0, :, :, 0:D]
    sc = mod_ref[0, :, :, D:2 * D]
    ga = mod_ref[0, 0, :, 2 * D:3 * D]
    h_ref[...] = _norm_mod(x_ref[...], nw_ref[0], sc, sh).reshape(T_REC, D)
    lb = lb_ref[0]
    hn = hn_ref[0]
    hs = (1, 2, 4, 8, 16, 32, 64)
    nchunk = T_REC // CHUNK

    def project(c, z_ref):
        rows = pl.ds(pl.multiple_of(c * CHUNK, CHUNK), CHUNK)
        z_ref[...] = _dot(h_ref[rows, :].astype(bf16), _unpack_rows(w_ref[0]))

    wcols = z0_ref.shape[1]
    piece_w = [768, 512] * (REC_HEADS // 2)
    assert sum(piece_w) == wcols
    piece_0 = [sum(piece_w[:i]) for i in range(REC_HEADS)]
    whole = slice(0, DK)

    def chunk(c, z_ref, z_next_ref):
        rows = pl.ds(pl.multiple_of(c * CHUNK, CHUNK), CHUNK)
        if z_next_ref is not None:
            c_next = jnp.minimum(c + 1, nchunk - 1)
            nrows = pl.ds(pl.multiple_of(c_next * CHUNK, CHUNK), CHUNK)
            h_next = h_ref[nrows, :].astype(bf16)
        mixed = []
        for hd in range(REC_HEADS):
            col = slice(hd * DK, (hd + 1) * DK)
            rq = z_ref[:, hd * DK:(hd + 1) * DK]
            rf = z_ref[:, D + hd * DK:D + (hd + 1) * DK]
            vr = z_ref[:, 2 * D + hd * DK:2 * D + (hd + 1) * DK]
            rg = z_ref[:, 3 * D + hd * DK:3 * D + (hd + 1) * DK]
            gr = z_ref[:, 4 * D + hd * DK:4 * D + (hd + 1) * DK]
            qr, kr, g = _hgrn_gates(rq, rf, lb[:, col])
            b = _chunk_prefix(g)
            b_last = b[CHUNK - 1:CHUNK, :]
            qe = (qr * jnp.exp2(b)).astype(bf16)
            khat = (kr * jnp.exp2(b_last - b)).astype(bf16)
            levels = [_level_operands(qr, kr, b, hsz) for hsz in hs]
            vb = vr.astype(bf16)
            s_old = s_ref[hd]
            att = _intra_scores(qr.astype(bf16), kr.astype(bf16), levels, hs, masks_ref, whole)
            if z_next_ref is not None:
                pc = slice(piece_0[hd], piece_0[hd] + piece_w[hd])
                z_next_ref[:, pc] = _dot(h_next, _unpack_rows(w_ref[0, :, pc]))
            o = _dot(jnp.concatenate([qe, att.astype(bf16)], axis=1),
                     jnp.concatenate([s_old.astype(bf16), vb], axis=0))
            dcol = _decay_column(b[CHUNK - SUBLANES:, :], SUBLANES - 1)
            s_ref[hd] = dcol * s_old + _dot_tn(khat, vb)
            o_rec = _head_norm_gate(o, hn[:, col], rg)
            mixed.append((a_ref[0, rows, col] + _sigmoid(gr) * o_rec).astype(bf16))
        y = _dot(jnp.concatenate(mixed, axis=1), _unpack_rows(wo_ref[0]))
        o_ref[0, rows, :] = x_ref[0, rows, :] + ga * y

    project(0, z0_ref)

    def pair(i, carry):
        chunk(2 * i, z0_ref, z1_ref)
        chunk(2 * i + 1, z1_ref, z0_ref)
        return carry

    lax.fori_loop(0, nchunk // 2, pair, 0)

    @pl.when(t == nt - 1)
    def _():
        st_ref[0] = s_ref[...]


def _rec_prompt_call(x, mod, mod_row0, nw, w_rec, lb, hn, a_mix, w_out, masks, layer):
    b, l, _ = x.shape
    wcols = w_rec.shape[-1]
    return pl.pallas_call(
        _rec_prompt_kernel,
        out_shape=(jax.ShapeDtypeStruct((b, l, D), f32),
                   jax.ShapeDtypeStruct((b, REC_HEADS, DK, DK), f32)),
        grid=(b, l // T_REC),
        in_specs=[
            pl.BlockSpec((1, T_REC, D), lambda i, t: (i, t, 0)),
            pl.BlockSpec((1, 1, 1, 6 * D), lambda i, t: (layer, mod_row0 + i, 0, 0)),
            _resident((1, 1, D), lambda i, t: (layer, 0, 0)),
            _resident((1, D // 2, wcols), lambda i, t: (layer, 0, 0)),
            _resident((1, 1, D), lambda i, t: (layer, 0, 0)),
            _resident((1, 1, D), lambda i, t: (layer, 0, 0)),
            pl.BlockSpec((1, T_REC, D), lambda i, t: (i, t, 0)),
            _resident((1, D // 2, D), lambda i, t: (layer, 0, 0)),
            _resident(masks.shape, lambda i, t: (0, 0, 0)),
        ],
        out_specs=(pl.BlockSpec((1, T_REC, D), lambda i, t: (i, t, 0)),
                   pl.BlockSpec((1, REC_HEADS, DK, DK), lambda i, t: (i, 0, 0, 0))),
        scratch_shapes=[pltpu.VMEM((T_REC, D), f32),
                        pltpu.VMEM((CHUNK, wcols), f32),
                        pltpu.VMEM((CHUNK, wcols), f32),
                        pltpu.VMEM((REC_HEADS, DK, DK), f32)],
        compiler_params=_cparams(2),
        name="rec_prompt",
    )(x, mod, nw, w_rec, lb, hn, a_mix, w_out, masks)


def _rec_sample_kernel(x_ref, mod_ref, nw_ref, w_ref, lb_ref, hn_ref, a_ref, wo_ref,
                       masks_ref, s0_ref, o_ref, s1_ref):
    sb, r, _ = x_ref.shape
    rows = sb * r
    x = x_ref[...]
    sh = mod_ref[0, :, :, 0:D]
    sc = mod_ref[0, :, :, D:2 * D]
    ga = mod_ref[0, :, :, 2 * D:3 * D]
    h = _norm_mod(x, nw_ref[0], sc, sh).reshape(rows, D).astype(bf16)
    z = _dot(h, _unpack_rows(w_ref[0]))
    qr, kr, g = _hgrn_gates(z[:, 0:D], z[:, D:2 * D], lb_ref[0])
    vr = z[:, 2 * D:3 * D]
    rg = z[:, 3 * D:4 * D]
    gr = z[:, 4 * D:5 * D]
    hn = hn_ref[0]

    sub = lax.broadcasted_iota(jnp.int32, (1, SUBLANES, D), 1)
    b3 = _tile_prefix(g.reshape(sb, r, D), sub)
    b = b3.reshape(rows, D)
    b_last3 = b3[:, r - 1:r, :]
    qe3 = (qr.reshape(sb, r, D) * jnp.exp2(b3)).astype(bf16)
    khat3 = (kr.reshape(sb, r, D) * jnp.exp2(b_last3 - b3)).astype(bf16)
    hs = (1, 2, 4)
    levels = [_level_operands(qr, kr, b, hsz) for hsz in hs]
    qr_b = qr.astype(bf16)
    kr_b = kr.astype(bf16)
    vb = vr.astype(bf16)
    vb3 = vb.reshape(sb, r, D)
    b_last = b_last3.reshape(sb, D)

    outs = []
    for hd in range(REC_HEADS):
        col = slice(hd * DK, (hd + 1) * DK)
        s_old = s0_ref[:, hd]
        att = _intra_scores(qr_b, kr_b, levels, hs, masks_ref, col)
        o_intra = _dot(att.astype(bf16), vb[:, col])
        o_inter = jnp.einsum('bqk,bkv->bqv', qe3[:, :, col], s_old.astype(bf16),
                             preferred_element_type=f32)
        upd = jnp.einsum('btk,btv->bkv', khat3[:, :, col], vb3[:, :, col],
                         preferred_element_type=f32)
        bl = b_last[:, col]
        for i in range(sb):
            s1_ref[i, hd] = _decay_column(bl, i) * s_old[i] + upd[i]
        o = o_inter.reshape(rows, DK) + o_intra
        outs.append(_head_norm_gate(o, hn[:, col], rg[:, col]))
    o_rec = jnp.concatenate(outs, axis=1)
    mixed = a_ref[...].reshape(rows, D) + _sigmoid(gr) * o_rec
    y = _dot(mixed.astype(bf16), _unpack_rows(wo_ref[0]))
    o_ref[...] = x + ga * y.reshape(sb, r, D)


def _rec_sample_call(x, mod, nw, w_rec, lb, hn, a_mix, w_out, masks, s0, layer):
    nb, r, _ = x.shape
    wcols = w_rec.shape[-1]
    blk3 = lambda i: (i, 0, 0)
    return pl.pallas_call(
        _rec_sample_kernel,
        out_shape=(jax.ShapeDtypeStruct((nb, r, D), f32),
                   jax.ShapeDtypeStruct((nb, REC_HEADS, DK, DK), f32)),
        grid=(nb // SB_REC,),
        in_specs=[
            pl.BlockSpec((SB_REC, r, D), blk3),
            pl.BlockSpec((1, SB_REC, 1, 6 * D), lambda i: (layer, i, 0, 0)),
            _resident((1, 1, D), lambda i: (layer, 0, 0)),
            _resident((1, D // 2, wcols), lambda i: (layer, 0, 0)),
            _resident((1, 1, D), lambda i: (layer, 0, 0)),
            _resident((1, 1, D), lambda i: (layer, 0, 0)),
            pl.BlockSpec((SB_REC, r, D), blk3),
            _resident((1, D // 2, D), lambda i: (layer, 0, 0)),
            _resident(masks.shape, lambda i: (0, 0, 0)),
            pl.BlockSpec((None, SB_REC, REC_HEADS, DK, DK), lambda i: (layer, i, 0, 0, 0)),
        ],
        out_specs=(pl.BlockSpec((SB_REC, r, D), blk3),
                   pl.BlockSpec((SB_REC, REC_HEADS, DK, DK), lambda i: (i, 0, 0, 0))),
        compiler_params=_cparams(1),
        name="rec_sample",
    )(x, mod, nw, w_rec, lb, hn, a_mix, w_out, masks, s0)


def _alibi_slopes():
    return 2.0 ** (-8.0 * jnp.arange(1, N_HEADS + 1, dtype=f32) / N_HEADS)


def _prompt_bias():
    t = np.arange(WINDOW)[:, None]
    c = np.arange(2 * WINDOW)[None, :]
    dist = WINDOW + t - c
    valid = (dist >= 0) & (dist <= WINDOW)
    valid = np.stack([valid, valid & (c >= WINDOW)])
    slopes = _alibi_slopes()[None, :, None, None]
    return jnp.where(jnp.asarray(valid)[:, None], -slopes * jnp.asarray(dist, f32), MASK_VALUE)


def _sample_bias(r, w, sb):
    t = np.arange(r)[:, None]
    c = np.arange(w)[None, :]
    dist_c = w + t - c
    valid_c = (dist_c >= 0) & (dist_c <= WINDOW)
    slopes = _alibi_slopes()
    bias_c = jnp.where(jnp.asarray(valid_c)[None], -slopes[:, None, None] * jnp.asarray(dist_c, f32),
                       MASK_VALUE)
    bq = np.arange(sb)[:, None, None, None]
    tq = np.arange(r)[None, :, None, None]
    bk = np.arange(sb)[None, None, :, None]
    tk = np.arange(r)[None, None, None, :]
    dist_n = np.broadcast_to(tq - tk, (sb, r, sb, r)).reshape(sb, r, sb * r)
    valid_n = np.broadcast_to((bq == bk) & (tq >= tk), (sb, r, sb, r)).reshape(sb, r, sb * r)
    bias_n = jnp.where(jnp.asarray(valid_n)[None],
                       -slopes[:, None, None, None] * jnp.asarray(dist_n, f32), MASK_VALUE)
    return bias_c, bias_n


def _level_masks(rows, hs):
    t = np.arange(rows)[:, None]
    s = np.arange(rows)[None, :]
    out = [t == s]
    for hsz in hs:
        same = (t // (2 * hsz)) == (s // (2 * hsz))
        out.append(same & ((t % (2 * hsz)) >= hsz) & ((s % (2 * hsz)) < hsz))
    return jnp.asarray(np.stack(out), f32)


def kernel(x_prompt, x_sample, cache_win_k, cache_win_v, state_hgrn, c_prompt, c_sample,
           norm_mix, norm_mlp, w_ada, b_ada, w_in, attn_sinks, hgrn_lb_logits, hgrn_norm,
           w_out, w_up, w_down, final_norm):
    depth = w_in.shape[0]
    nb_p, seq, _ = x_prompt.shape
    nb_s, dec, _ = x_sample.shape
    w_buf = cache_win_k.shape[2]
    kvw = N_KV * HD
    assert seq % T_ATT == 0 and seq % T_REC == 0 and seq % T_MLP == 0
    assert T_REC % (2 * CHUNK) == 0
    assert dec == SUBLANES and w_buf == WINDOW
    assert nb_s % SB_ATT == 0 and nb_s % SB_REC == 0 and nb_s % SB_MLP == 0

    pad = (-(nb_s + nb_p)) % SUBLANES
    c_all = jnp.concatenate([c_sample, c_prompt, jnp.zeros((pad, D), f32)], axis=0)
    n_c = c_all.shape[0]
    n_mod = 6 * D
    mod_cols = 1536
    mod = pl.pallas_call(
        _mod_kernel,
        out_shape=jax.ShapeDtypeStruct((depth, n_c, n_mod), f32),
        grid=(depth, n_mod // mod_cols),
        in_specs=[
            _resident((n_c, D), lambda l, j: (0, 0)),
            pl.BlockSpec((1, D, mod_cols), lambda l, j: (l, 0, j)),
            pl.BlockSpec((1, 1, mod_cols), lambda l, j: (l, 0, j)),
        ],
        out_specs=pl.BlockSpec((1, n_c, mod_cols), lambda l, j: (l, 0, j)),
        compiler_params=_cparams(2),
        name="adaln_mod",
    )(c_all, w_ada, b_ada.reshape(depth, 1, n_mod))
    mod = mod.reshape(depth, n_c, 1, n_mod)
    prompt_row0 = nb_s

    lb_all = pl.pallas_call(
        _lb_kernel,
        out_shape=jax.ShapeDtypeStruct(hgrn_lb_logits.shape, f32),
        name="hgrn_lower_bound",
    )(hgrn_lb_logits).reshape(depth, 1, D)

    v_end = D + 2 * kvw
    rec_end = v_end + 4 * D
    w_attn = _pack_rows(jnp.concatenate(
        [w_in[:, :, :v_end], w_in[:, :, rec_end:rec_end + D]], axis=-1))
    w_rec = _pack_rows(jnp.concatenate(
        [w_in[:, :, v_end:rec_end], w_in[:, :, rec_end + D:]], axis=-1))
    w_out_b = _pack_rows(w_out)
    w_up_b = _pack_rows(w_up)
    w_dn_b = _pack_rows(w_down)

    nmix = norm_mix.reshape(depth, 1, D)
    nmlp = norm_mlp.reshape(depth, 1, D)
    hnorm = hgrn_norm.reshape(depth, 1, D)
    fnorm = final_norm.reshape(1, D)

    bias_p = _prompt_bias()
    bias_c, bias_n = _sample_bias(dec, w_buf, SB_ATT)
    masks_p = _level_masks(CHUNK, (1, 2, 4, 8, 16, 32, 64))
    masks_s = _level_masks(SB_REC * dec, (1, 2, 4))

    ck_all = cache_win_k.reshape(depth, nb_s, w_buf, kvw)
    cv_all = cache_win_v.reshape(depth, nb_s, w_buf, kvw)

    xp, xs = x_prompt, x_sample
    pk, pv, ps, sk, sv, ss = [], [], [], [], [], []
    for l in range(depth):
        final = l == depth - 1
        a_p, k_p, v_p = _attn_prompt_call(xp, mod, prompt_row0, nmix, w_attn, attn_sinks,
                                          bias_p, l)
        xp, s_p = _rec_prompt_call(xp, mod, prompt_row0, nmix, w_rec, lb_all, hnorm, a_p,
                                   w_out_b, masks_p, l)
        xp = _mlp_prompt(xp, mod, prompt_row0, nmlp, w_up_b, w_dn_b, fnorm, l, final)

        a_s, k_s, v_s = _attn_sample_call(xs, mod, nmix, w_attn, attn_sinks, ck_all, cv_all,
                                          bias_c, bias_n, l)
        xs, s_s = _rec_sample_call(xs, mod, nmix, w_rec, lb_all, hnorm, a_s, w_out_b,
                                   masks_s, state_hgrn, l)
        xs = _mlp_call(xs, mod, 0, nmlp, w_up_b, w_dn_b, fnorm, l, final, SB_MLP)

        pk.append(k_p.reshape(nb_p, w_buf, N_KV, HD))
        pv.append(v_p.reshape(nb_p, w_buf, N_KV, HD))
        ps.append(s_p)
        sk.append(k_s.reshape(nb_s, w_buf, N_KV, HD))
        sv.append(v_s.reshape(nb_s, w_buf, N_KV, HD))
        ss.append(s_s)

    return (xp, xs, jnp.stack(pk), jnp.stack(pv), jnp.stack(ps),
            jnp.stack(sk), jnp.stack(sv), jnp.stack(ss))


def _mlp_prompt(x, mod, mod_row0, nw, wup, wdn, fn, layer, final):
    b, l, _ = x.shape
    return pl.pallas_call(
        functools.partial(_mlp_kernel, final=final),
        out_shape=jax.ShapeDtypeStruct(x.shape, f32),
        grid=(b, l // T_MLP),
        in_specs=[
            pl.BlockSpec((1, T_MLP, D), lambda i, t: (i, t, 0)),
            pl.BlockSpec((1, 1, 1, 6 * D), lambda i, t: (layer, mod_row0 + i, 0, 0)),
            _resident((1, 1, D), lambda i, t: (layer, 0, 0)),
            _resident((1, D // 2, D_FF), lambda i, t: (layer, 0, 0)),
            _resident((1, D_FF // 2, D), lambda i, t: (layer, 0, 0)),
            _resident((1, D), lambda i, t: (0, 0)),
        ],
        out_specs=pl.BlockSpec((1, T_MLP, D), lambda i, t: (i, t, 0)),
        compiler_params=_cparams(2),
        name="mlp_prompt",
    )(x, mod, nw, wup, wdn, fn)
```

```python
import functools

import numpy as np
import jax
import jax.numpy as jnp
from jax import lax
from jax.experimental import pallas as pl
from jax.experimental.pallas import tpu as pltpu

f32 = jnp.float32
bf16 = jnp.bfloat16

D = 1024
N_HEADS = 16
N_KV = 4
HD = 64
WINDOW = 128
REC_HEADS = 8
DK = 128
D_FF = 4 * D
EPS = 1e-6
MASK_VALUE = -1e30
LANES = 128
SUBLANES = 8
VMEM_LIMIT = 56 * 1024 * 1024

T_ATT = 1024
T_REC = 1024
T_MLP = 512
CHUNK = 128
SB_ATT = 16
SB_REC = 8
SB_MLP = 64


def _cparams(n_axes):
    return pltpu.CompilerParams(
        dimension_semantics=("arbitrary",) * n_axes, vmem_limit_bytes=VMEM_LIMIT)


def _resident(shape, index_map):
    return pl.BlockSpec(shape, index_map, pipeline_mode=pl.Buffered(1))


def _dot(a, b):
    return jnp.dot(a, b, preferred_element_type=f32)


PACK_ROWS = 256


def _pack_kernel(w_ref, *o_refs, col_groups):
    for o_ref, groups in zip(o_refs, col_groups):
        parts = [w_ref[0, :, a:b] for a, b in groups]
        w = parts[0] if len(parts) == 1 else jnp.concatenate(parts, axis=1)
        o_ref[0] = pltpu.bitcast(w.astype(bf16), jnp.uint32)


def _pack_rows(w, col_groups=None):
    depth, k, n = w.shape
    if col_groups is None:
        col_groups = (((0, n),),)
    widths = [sum(b - a for a, b in g) for g in col_groups]
    outs = pl.pallas_call(
        functools.partial(_pack_kernel, col_groups=col_groups),
        out_shape=tuple(jax.ShapeDtypeStruct((depth, k // 2, wd), jnp.uint32) for wd in widths),
        grid=(depth, k // PACK_ROWS),
        in_specs=[pl.BlockSpec((1, PACK_ROWS, n), lambda l, i: (l, i, 0))],
        out_specs=tuple(pl.BlockSpec((1, PACK_ROWS // 2, wd), lambda l, i: (l, i, 0))
                        for wd in widths),
        compiler_params=_cparams(2),
        name="pack_weights",
    )(w)
    return outs[0] if len(outs) == 1 else outs


def _unpack_rows(w):
    return pltpu.bitcast(w, bf16)


def _dot_nt(a, b):
    return lax.dot_general(a, b, (((1,), (1,)), ((), ())), preferred_element_type=f32)


def _dot_tn(a, b):
    return lax.dot_general(a, b, (((0,), (0,)), ((), ())), preferred_element_type=f32)


def _sigmoid(x):
    return 0.5 + 0.5 * jnp.tanh(0.5 * x)


def _silu(x):
    hx = 0.5 * x
    return hx + hx * jnp.tanh(hx)


def _norm_mod(x, nw, sc, sh):
    ms = jnp.mean(x * x, axis=-1, keepdims=True)
    y = x * lax.rsqrt(ms + EPS) * nw
    return y * (1.0 + sc) + sh


def _mod_kernel(c_ref, w_ref, b_ref, o_ref):
    c = c_ref[...]
    s = _silu(c).astype(bf16)
    o_ref[0] = _dot(s, w_ref[0].astype(bf16)) + b_ref[0]


def _lb_kernel(l_ref, o_ref):
    rows = [l_ref[i:i + 1, :] for i in range(l_ref.shape[0])]
    m = functools.reduce(jnp.maximum, rows)
    e = [jnp.exp(r - m) for r in rows]
    tot = functools.reduce(lambda a, b: a + b, e)
    p = [x / tot for x in e]
    run = p[0]
    o_ref[0:1, :] = run - p[0]
    for i in range(1, len(p)):
        run = run + p[i]
        o_ref[i:i + 1, :] = run - p[0]


def _mlp_kernel(x_ref, mod_ref, nw_ref, wup_ref, wdn_ref, fn_ref, o_ref, *, final):
    x = x_ref[...]
    g, r, _ = x.shape
    sh = mod_ref[0, :, :, 3 * D:4 * D]
    sc = mod_ref[0, :, :, 4 * D:5 * D]
    gm = mod_ref[0, :, :, 5 * D:6 * D]
    h = _norm_mod(x, nw_ref[0], sc, sh).reshape(g * r, D).astype(bf16)
    acc = jnp.zeros((g * r, D), f32)
    for c in range(D_FF // D):
        u = _dot(h, _unpack_rows(wup_ref[0, :, c * D:(c + 1) * D]))
        u = jnp.square(jnp.maximum(u, 0.0)).astype(bf16)
        acc = acc + _dot(u, _unpack_rows(wdn_ref[0, c * D // 2:(c + 1) * D // 2, :]))
    y = x + gm * acc.reshape(g, r, D)
    if final:
        ms = jnp.mean(y * y, axis=-1, keepdims=True)
        y = y * lax.rsqrt(ms + EPS) * fn_ref[...]
    o_ref[...] = y


def _mlp_call(x, mod, mod_row0, nw, wup, wdn, fn, layer, final, gblk):
    ng, r, _ = x.shape
    grid = (ng // gblk,)
    mrow = mod_row0 // gblk
    return pl.pallas_call(
        functools.partial(_mlp_kernel, final=final),
        out_shape=jax.ShapeDtypeStruct(x.shape, f32),
        grid=grid,
        in_specs=[
            pl.BlockSpec((gblk, r, D), lambda i: (i, 0, 0)),
            pl.BlockSpec((1, gblk, 1, 6 * D), lambda i: (layer, mrow + i, 0, 0)),
            _resident((1, 1, D), lambda i: (layer, 0, 0)),
            _resident((1, D // 2, D_FF), lambda i: (layer, 0, 0)),
            _resident((1, D_FF // 2, D), lambda i: (layer, 0, 0)),
            _resident((1, D), lambda i: (0, 0)),
        ],
        out_specs=pl.BlockSpec((gblk, r, D), lambda i: (i, 0, 0)),
        compiler_params=_cparams(1),
        name="mlp",
    )(x, mod, nw, wup, wdn, fn)


def _lane_lt_half(shape):
    return lax.broadcasted_iota(jnp.int32, shape, len(shape) - 1) < HD


def _q_for_head(zq, h, lower):
    blk = zq[:, (h // 2) * LANES:(h // 2 + 1) * LANES]
    own, tgt = h % 2, (h // 4) % 2
    if own != tgt:
        blk = pltpu.roll(blk, HD, 1)
    keep = lower if tgt == 0 else jnp.logical_not(lower)
    return jnp.where(keep, blk, 0.0).astype(bf16)


def _heads_to_natural(o_heads, lower):
    cols = []
    for jn in range(N_HEADS // 2):
        tgt = (jn // 2) % 2
        a, b = o_heads[2 * jn], o_heads[2 * jn + 1]
        if tgt == 1:
            a = pltpu.roll(a, HD, 1)
        else:
            b = pltpu.roll(b, HD, 1)
        cols.append(jnp.where(lower, a, b))
    return jnp.concatenate(cols, axis=1)


def _softmax_parts(s, sink):
    m = jnp.maximum(jnp.max(s, axis=-1, keepdims=True), sink)
    p = jnp.exp(s - m)
    l = jnp.sum(p, axis=-1, keepdims=True) + jnp.exp(sink - m)
    return p, l


def _attn_prompt_kernel(sinks_ref, x_ref, mod_ref, nw_ref, w_ref, bias_ref,
                        a_ref, kw_ref, vw_ref, h_ref, z0_ref, z1_ref, kprev_ref, vprev_ref,
                        *, layer):
    t = pl.program_id(1)
    nt = pl.num_programs(1)

    @pl.when(t == 0)
    def _():
        kprev_ref[...] = jnp.zeros_like(kprev_ref)
        vprev_ref[...] = jnp.zeros_like(vprev_ref)

    sh = mod_ref[0, :, :, 0:D]
    sc = mod_ref[0, :, :, D:2 * D]
    h_ref[...] = _norm_mod(x_ref[...], nw_ref[0], sc, sh).reshape(T_ATT, D)
    nblk = T_ATT // WINDOW
    kvw = N_KV * HD
    wcols = z0_ref.shape[1]
    lower = _lane_lt_half((WINDOW, LANES))

    def project(j, z_ref):
        rows = pl.ds(pl.multiple_of(j * WINDOW, WINDOW), WINDOW)
        z_ref[...] = _dot(h_ref[rows, :].astype(bf16), _unpack_rows(w_ref[0]))

    def block(j, z_ref, z_next_ref):
        rows = pl.ds(pl.multiple_of(j * WINDOW, WINDOW), WINDOW)
        j_next = jnp.minimum(j + 1, nblk - 1)
        h_next = h_ref[pl.ds(pl.multiple_of(j_next * WINDOW, WINDOW), WINDOW), :].astype(bf16)
        zq = z_ref[:, 0:D] * (HD ** -0.5)
        k = z_ref[:, D:D + kvw]
        v = z_ref[:, D + kvw:D + 2 * kvw]
        ga = z_ref[:, D + 2 * kvw:]
        kcat = jnp.concatenate([kprev_ref[...].astype(bf16), k.astype(bf16)], axis=0)
        vcat = jnp.concatenate([vprev_ref[...].astype(bf16), v.astype(bf16)], axis=0)
        first = jnp.logical_and(t == 0, j == 0).astype(jnp.int32)
        o_heads = [None] * N_HEADS
        for p in range(2):
            kp = kcat[:, p * LANES:(p + 1) * LANES]
            vp = vcat[:, p * LANES:(p + 1) * LANES]
            heads = range(8 * p, 8 * p + 8)
            qs = jnp.concatenate([_q_for_head(zq, hh, lower) for hh in heads], axis=0)
            s_all = _dot_nt(qs, kp)
            pc = slice(p * (wcols // 2), (p + 1) * (wcols // 2))
            z_next_ref[:, pc] = _dot(h_next, _unpack_rows(w_ref[0, :, pc]))
            probs, ls = [], []
            for i, hh in enumerate(heads):
                s = s_all[i * WINDOW:(i + 1) * WINDOW] + bias_ref[first, hh]
                pr, l = _softmax_parts(s, sinks_ref[layer, hh])
                probs.append(pr.astype(bf16))
                ls.append(l)
            o_all = _dot(jnp.concatenate(probs, axis=0), vp)
            for i, hh in enumerate(heads):
                o_heads[hh] = o_all[i * WINDOW:(i + 1) * WINDOW] / ls[i]
        a_ref[0, rows, :] = _sigmoid(ga) * _heads_to_natural(o_heads, lower)
        kprev_ref[...] = k
        vprev_ref[...] = v

    project(0, z0_ref)

    def pair(i, carry):
        block(2 * i, z0_ref, z1_ref)
        block(2 * i + 1, z1_ref, z0_ref)
        return carry

    lax.fori_loop(0, nblk // 2, pair, 0)

    @pl.when(t == nt - 1)
    def _():
        kw_ref[0] = kprev_ref[...]
        vw_ref[0] = vprev_ref[...]


def _attn_prompt_call(x, mod, mod_row0, nw, w_attn, sinks, bias, layer):
    b, l, _ = x.shape
    wcols = w_attn.shape[-1]
    kvw = N_KV * HD
    return pl.pallas_call(
        functools.partial(_attn_prompt_kernel, layer=layer),
        out_shape=(jax.ShapeDtypeStruct((b, l, D), f32),
                   jax.ShapeDtypeStruct((b, WINDOW, kvw), f32),
                   jax.ShapeDtypeStruct((b, WINDOW, kvw), f32)),
        grid=(b, l // T_ATT),
        in_specs=[
            pl.BlockSpec(memory_space=pltpu.SMEM),
            pl.BlockSpec((1, T_ATT, D), lambda i, t: (i, t, 0)),
            pl.BlockSpec((1, 1, 1, 6 * D), lambda i, t: (layer, mod_row0 + i, 0, 0)),
            _resident((1, 1, D), lambda i, t: (layer, 0, 0)),
            _resident((1, D // 2, wcols), lambda i, t: (layer, 0, 0)),
            _resident(bias.shape, lambda i, t: (0, 0, 0, 0)),
        ],
        out_specs=(pl.BlockSpec((1, T_ATT, D), lambda i, t: (i, t, 0)),
                   pl.BlockSpec((1, WINDOW, kvw), lambda i, t: (i, 0, 0)),
                   pl.BlockSpec((1, WINDOW, kvw), lambda i, t: (i, 0, 0))),
        scratch_shapes=[pltpu.VMEM((T_ATT, D), f32),
                        pltpu.VMEM((WINDOW, wcols), f32),
                        pltpu.VMEM((WINDOW, wcols), f32),
                        pltpu.VMEM((WINDOW, kvw), f32),
                        pltpu.VMEM((WINDOW, kvw), f32)],
        compiler_params=_cparams(2),
        name="attn_prompt",
    )(sinks, x, mod, nw, w_attn, bias)


def _attn_sample_kernel(sinks_ref, x_ref, mod_ref, nw_ref, w_ref, ck_ref, cv_ref,
                        bc_ref, bn_ref, nk_all_ref, nv_all_ref, a_ref, nk_ref, nv_ref,
                        *, layer):
    del nk_all_ref, nv_all_ref
    sb, r, _ = x_ref.shape
    rows = sb * r
    x = x_ref[...]
    sh = mod_ref[0, :, :, 0:D]
    sc = mod_ref[0, :, :, D:2 * D]
    h = _norm_mod(x, nw_ref[0], sc, sh).reshape(rows, D).astype(bf16)
    z = _dot(h, _unpack_rows(w_ref[0]))
    zq = z[:, 0:D] * (HD ** -0.5)
    kvw = N_KV * HD
    k = z[:, D:D + kvw]
    v = z[:, D + kvw:D + 2 * kvw]
    ga = z[:, D + 2 * kvw:]

    ck = ck_ref[...]
    cv = cv_ref[...]
    w = ck.shape[1]
    nk_ref[:, 0:w - r, :] = ck[:, r:, :]
    nk_ref[:, w - r:, :] = k.reshape(sb, r, kvw)
    nv_ref[:, 0:w - r, :] = cv[:, r:, :]
    nv_ref[:, w - r:, :] = v.reshape(sb, r, kvw)

    ckb = ck.astype(bf16)
    cvb = cv.astype(bf16)
    kb = k.astype(bf16)
    vb = v.astype(bf16)

    lower = _lane_lt_half((rows, LANES))
    o_heads = [None] * N_HEADS
    for p in range(2):
        heads = range(8 * p, 8 * p + 8)
        qs = jnp.concatenate(
            [_q_for_head(zq, hh, lower).reshape(sb, r, LANES) for hh in heads], axis=1)
        kc = ckb[:, :, p * LANES:(p + 1) * LANES]
        vc = cvb[:, :, p * LANES:(p + 1) * LANES]
        kn = kb[:, p * LANES:(p + 1) * LANES]
        vn = vb[:, p * LANES:(p + 1) * LANES]
        s_c = jnp.einsum('bqd,bkd->bqk', qs, kc, preferred_element_type=f32)
        s_n = _dot_nt(qs.reshape(sb * 8 * r, LANES), kn).reshape(sb, 8 * r, rows)
        pc_parts, pn_parts, ls = [], [], []
        for i, hh in enumerate(heads):
            sl = slice(i * r, (i + 1) * r)
            sink = sinks_ref[layer, hh]
            sc_h = s_c[:, sl, :] + bc_ref[hh]
            sn_h = s_n[:, sl, :] + bn_ref[hh]
            m = jnp.maximum(jnp.maximum(jnp.max(sc_h, axis=-1, keepdims=True),
                                        jnp.max(sn_h, axis=-1, keepdims=True)), sink)
            pc = jnp.exp(sc_h - m)
            pn = jnp.exp(sn_h - m)
            l = (jnp.sum(pc, axis=-1, keepdims=True) + jnp.sum(pn, axis=-1, keepdims=True)
                 + jnp.exp(sink - m))
            pc_parts.append(pc.astype(bf16))
            pn_parts.append(pn.astype(bf16))
            ls.append(l)
        pc_all = jnp.concatenate(pc_parts, axis=1)
        pn_all = jnp.concatenate(pn_parts, axis=1)
        o_all = jnp.einsum('bqk,bkd->bqd', pc_all, vc, preferred_element_type=f32)
        o_all = o_all + _dot(pn_all.reshape(sb * 8 * r, rows), vn).reshape(sb, 8 * r, LANES)
        for i, hh in enumerate(heads):
            sl = slice(i * r, (i + 1) * r)
            o_heads[hh] = (o_all[:, sl, :] / ls[i]).reshape(rows, LANES)
    o_attn = _heads_to_natural(o_heads, lower)
    a_ref[...] = (_sigmoid(ga) * o_attn).reshape(sb, r, D)


def _attn_sample_call(x, mod, nw, w_attn, sinks, ck, cv, bias_c, bias_n, nk_all, nv_all,
                      layer):
    nb, r, _ = x.shape
    wcols = w_attn.shape[-1]
    kvw = N_KV * HD
    w = ck.shape[2]
    blk3 = lambda i: (i, 0, 0)
    return pl.pallas_call(
        functools.partial(_attn_sample_kernel, layer=layer),
        out_shape=(jax.ShapeDtypeStruct((nb, r, D), f32),
                   jax.ShapeDtypeStruct(nk_all.shape, f32),
                   jax.ShapeDtypeStruct(nv_all.shape, f32)),
        grid=(nb // SB_ATT,),
        input_output_aliases={9: 1, 10: 2},
        in_specs=[
            pl.BlockSpec(memory_space=pltpu.SMEM),
            pl.BlockSpec((SB_ATT, r, D), blk3),
            pl.BlockSpec((1, SB_ATT, 1, 6 * D), lambda i: (layer, i, 0, 0)),
            _resident((1, 1, D), lambda i: (layer, 0, 0)),
            _resident((1, D // 2, wcols), lambda i: (layer, 0, 0)),
            pl.BlockSpec((None, SB_ATT, w, kvw), lambda i: (layer, i, 0, 0)),
            pl.BlockSpec((None, SB_ATT, w, kvw), lambda i: (layer, i, 0, 0)),
            _resident(bias_c.shape, lambda i: (0, 0, 0)),
            _resident(bias_n.shape, lambda i: (0, 0, 0, 0)),
            pl.BlockSpec(memory_space=pl.ANY),
            pl.BlockSpec(memory_space=pl.ANY),
        ],
        out_specs=(pl.BlockSpec((SB_ATT, r, D), blk3),
                   pl.BlockSpec((None, SB_ATT, w, kvw), lambda i: (layer, i, 0, 0)),
                   pl.BlockSpec((None, SB_ATT, w, kvw), lambda i: (layer, i, 0, 0))),
        compiler_params=_cparams(1),
        name="attn_sample",
    )(sinks, x, mod, nw, w_attn, ck, cv, bias_c, bias_n, nk_all, nv_all)


def _hgrn_gates(rq, rf, lb):
    c1 = 0.5 * (1.0 - lb)
    u = c1 * jnp.tanh(0.5 * rf)
    g2 = jnp.log2((0.5 * (1.0 + lb)) + u)
    kr = c1 - u
    hq = 0.5 * rq
    qr = (hq * (DK ** -0.5)) * (1.0 + jnp.tanh(hq))
    return qr, kr, g2


def _tile_prefix(g3, sub):
    s = g3
    for shift in (1, 2, 4):
        s = s + jnp.where(sub >= shift, pltpu.roll(s, shift, 1), 0.0)
    return s


def _chunk_prefix(g):
    r, c = g.shape
    n = r // SUBLANES
    sub = lax.broadcasted_iota(jnp.int32, (1, SUBLANES, c), 1)
    p8 = _tile_prefix(g.reshape(n, SUBLANES, c), sub)
    tiles = [p8[0]]
    run = p8[0, SUBLANES - 1:SUBLANES, :]
    for i in range(1, n):
        ti = p8[i] + run
        tiles.append(ti)
        run = ti[SUBLANES - 1:SUBLANES, :]
    return jnp.concatenate(tiles, axis=0)


def _level_operands(qr, kr, b, hsz):
    r, c = b.shape
    if hsz >= SUBLANES:
        nb = r // (2 * hsz)
        b4 = b.reshape(nb, 2, hsz, c)
        bm = b4[:, 0, hsz - 1:hsz, :]
        e_lo = jnp.exp2(bm - b4[:, 0])
        e_up = jnp.exp2(b4[:, 1] - bm)
        e = jnp.stack([e_lo, e_up], axis=1).reshape(r, c)
        lhs = (qr.reshape(nb, 2, hsz, c)[:, 1] * e_up).reshape(nb * hsz, c)
        return lhs.astype(bf16), (kr * e).astype(bf16)
    n = r // SUBLANES
    b3 = b.reshape(n, SUBLANES, c)
    sub = lax.broadcasted_iota(jnp.int32, (1, SUBLANES, c), 1)
    if hsz == 4:
        bm = b3[:, 3:4, :]
    elif hsz == 2:
        bm = jnp.where(sub < 4, b3[:, 1:2, :], b3[:, 5:6, :])
    else:
        bm = jnp.where((sub & 1) == 0, b3, pltpu.roll(b3, 1, 1))
    d = (b3 - bm).reshape(r, c)
    upper = (sub & hsz) != 0
    src = jnp.where(upper, qr.reshape(n, SUBLANES, c), kr.reshape(n, SUBLANES, c))
    nd = lax.bitcast_convert_type(
        lax.bitcast_convert_type(d, jnp.uint32) | jnp.uint32(0x80000000), f32)
    w = (src.reshape(r, c) * jnp.exp2(nd)).astype(bf16)
    return w, w


def _intra_scores(qr_b, kr_b, levels, hs, masks_ref, col):
    r = qr_b.shape[0]
    att = masks_ref[0] * _dot_nt(qr_b[:, col], kr_b[:, col])
    tiled = []
    for i, ((lhs, rhs), hsz) in enumerate(zip(levels, hs)):
        if hsz < SUBLANES:
            att = att + masks_ref[i + 1] * _dot_nt(lhs[:, col], rhs[:, col])
        else:
            tiled.append((i + 1, hsz, _dot_nt(lhs[:, col], rhs[:, col])))
    if not tiled:
        return att
    tiles = [att[j * SUBLANES:(j + 1) * SUBLANES] for j in range(r // SUBLANES)]
    for idx, hsz, m in tiled:
        row = 0
        for blk in range(r // (2 * hsz)):
            for j in range(hsz // SUBLANES):
                t0 = blk * 2 * hsz + hsz + j * SUBLANES
                tiles[t0 // SUBLANES] = (tiles[t0 // SUBLANES]
                                         + masks_ref[idx, t0:t0 + SUBLANES, :]
                                         * m[row:row + SUBLANES])
                row += SUBLANES
    return jnp.concatenate(tiles, axis=0)


def _head_norm_gate(o, nw, rg):
    ms = jnp.mean(o * o, axis=-1, keepdims=True)
    return o * lax.rsqrt(ms + EPS) * nw * _silu(rg)


def _decay_column(b_last_tile, col_idx):
    tr = b_last_tile.T
    return jnp.broadcast_to(jnp.exp2(tr[:, col_idx:col_idx + 1]), (DK, DK))


def _rec_prompt_kernel(x_ref, mod_ref, nw_ref, w_ref, lb_ref, hn_ref, a_ref, wo_ref,
                       masks_ref, o_ref, st_ref, h_ref, z0_ref, z1_ref, s_ref):
    t = pl.program_id(1)
    nt = pl.num_programs(1)

    @pl.when(t == 0)
    def _():
        s_ref[...] = jnp.zeros_like(s_ref)

    sh = mod_ref[0, :, :, 0:D]
    sc = mod_ref[0, :, :, D:2 * D]
    ga = mod_ref[0, 0, :, 2 * D:3 * D]
    h_ref[...] = _norm_mod(x_ref[...], nw_ref[0], sc, sh).reshape(T_REC, D)
    lb = lb_ref[0]
    hn = hn_ref[0]
    hs = (1, 2, 4, 8, 16, 32, 64)
    nchunk = T_REC // CHUNK

    def project(c, z_ref):
        rows = pl.ds(pl.multiple_of(c * CHUNK, CHUNK), CHUNK)
        z_ref[...] = _dot(h_ref[rows, :].astype(bf16), _unpack_rows(w_ref[0]))

    wcols = z0_ref.shape[1]
    piece_w = [768, 512] * (REC_HEADS // 2)
    assert sum(piece_w) == wcols
    piece_0 = [sum(piece_w[:i]) for i in range(REC_HEADS)]
    whole = slice(0, DK)

    def chunk(c, z_ref, z_next_ref):
        rows = pl.ds(pl.multiple_of(c * CHUNK, CHUNK), CHUNK)
        if z_next_ref is not None:
            c_next = jnp.minimum(c + 1, nchunk - 1)
            nrows = pl.ds(pl.multiple_of(c_next * CHUNK, CHUNK), CHUNK)
            h_next = h_ref[nrows, :].astype(bf16)
        def scores(hd):
            col = slice(hd * DK, (hd + 1) * DK)
            rq = z_ref[:, hd * DK:(hd + 1) * DK]
            rf = z_ref[:, D + hd * DK:D + (hd + 1) * DK]
            vr = z_ref[:, 2 * D + hd * DK:2 * D + (hd + 1) * DK]
            qr, kr, g = _hgrn_gates(rq, rf, lb[:, col])
            b = _chunk_prefix(g)
            b_last = b[CHUNK - 1:CHUNK, :]
            qe = (qr * jnp.exp2(b)).astype(bf16)
            khat = (kr * jnp.exp2(b_last - b)).astype(bf16)
            levels = [_level_operands(qr, kr, b, hsz) for hsz in hs]
            att = _intra_scores(qr.astype(bf16), kr.astype(bf16), levels, hs, masks_ref, whole)
            return qe, att.astype(bf16), khat, vr.astype(bf16), b[CHUNK - SUBLANES:, :]

        def finish(hd, qe, att, khat, vb, b_tail):
            col = slice(hd * DK, (hd + 1) * DK)
            rg = z_ref[:, 3 * D + hd * DK:3 * D + (hd + 1) * DK]
            gr = z_ref[:, 4 * D + hd * DK:4 * D + (hd + 1) * DK]
            s_old = s_ref[hd]
            o = _dot(jnp.concatenate([qe, att], axis=1),
                     jnp.concatenate([s_old.astype(bf16), vb], axis=0))
            dcol = _decay_column(b_tail, SUBLANES - 1)
            s_ref[hd] = dcol * s_old + _dot_tn(khat, vb)
            o_rec = _head_norm_gate(o, hn[:, col], rg)
            return (a_ref[0, rows, col] + _sigmoid(gr) * o_rec).astype(bf16)

        mixed = []
        pending = None
        for hd in range(REC_HEADS):
            cur = scores(hd)
            if z_next_ref is not None:
                pc = slice(piece_0[hd], piece_0[hd] + piece_w[hd])
                z_next_ref[:, pc] = _dot(h_next, _unpack_rows(w_ref[0, :, pc]))
            if pending is not None:
                mixed.append(finish(hd - 1, *pending))
            pending = cur
        mixed.append(finish(REC_HEADS - 1, *pending))
        y = _dot(jnp.concatenate(mixed, axis=1), _unpack_rows(wo_ref[0]))
        o_ref[0, rows, :] = x_ref[0, rows, :] + ga * y

    project(0, z0_ref)

    def pair(i, carry):
        chunk(2 * i, z0_ref, z1_ref)
        chunk(2 * i + 1, z1_ref, z0_ref)
        return carry

    lax.fori_loop(0, nchunk // 2, pair, 0)

    @pl.when(t == nt - 1)
    def _():
        st_ref[0] = s_ref[...]


def _rec_prompt_call(x, mod, mod_row0, nw, w_rec, lb, hn, a_mix, w_out, masks, layer):
    b, l, _ = x.shape
    wcols = w_rec.shape[-1]
    return pl.pallas_call(
        _rec_prompt_kernel,
        out_shape=(jax.ShapeDtypeStruct((b, l, D), f32),
                   jax.ShapeDtypeStruct((b, REC_HEADS, DK, DK), f32)),
        grid=(b, l // T_REC),
        in_specs=[
            pl.BlockSpec((1, T_REC, D), lambda i, t: (i, t, 0)),
            pl.BlockSpec((1, 1, 1, 6 * D), lambda i, t: (layer, mod_row0 + i, 0, 0)),
            _resident((1, 1, D), lambda i, t: (layer, 0, 0)),
            _resident((1, D // 2, wcols), lambda i, t: (layer, 0, 0)),
            _resident((1, 1, D), lambda i, t: (layer, 0, 0)),
            _resident((1, 1, D), lambda i, t: (layer, 0, 0)),
            pl.BlockSpec((1, T_REC, D), lambda i, t: (i, t, 0)),
            _resident((1, D // 2, D), lambda i, t: (layer, 0, 0)),
            _resident(masks.shape, lambda i, t: (0, 0, 0)),
        ],
        out_specs=(pl.BlockSpec((1, T_REC, D), lambda i, t: (i, t, 0)),
                   pl.BlockSpec((1, REC_HEADS, DK, DK), lambda i, t: (i, 0, 0, 0))),
        scratch_shapes=[pltpu.VMEM((T_REC, D), f32),
                        pltpu.VMEM((CHUNK, wcols), f32),
                        pltpu.VMEM((CHUNK, wcols), f32),
                        pltpu.VMEM((REC_HEADS, DK, DK), f32)],
        compiler_params=_cparams(2),
        name="rec_prompt",
    )(x, mod, nw, w_rec, lb, hn, a_mix, w_out, masks)


def _rec_sample_kernel(x_ref, mod_ref, nw_ref, w_ref, lb_ref, hn_ref, a_ref, wo_ref,
                       masks_ref, s0_ref, s1_all_ref, o_ref, s1_ref):
    del s1_all_ref
    sb, r, _ = x_ref.shape
    rows = sb * r
    x = x_ref[...]
    sh = mod_ref[0, :, :, 0:D]
    sc = mod_ref[0, :, :, D:2 * D]
    ga = mod_ref[0, :, :, 2 * D:3 * D]
    h = _norm_mod(x, nw_ref[0], sc, sh).reshape(rows, D).astype(bf16)
    z = _dot(h, _unpack_rows(w_ref[0]))
    qr, kr, g = _hgrn_gates(z[:, 0:D], z[:, D:2 * D], lb_ref[0])
    vr = z[:, 2 * D:3 * D]
    rg = z[:, 3 * D:4 * D]
    gr = z[:, 4 * D:5 * D]
    hn = hn_ref[0]

    sub = lax.broadcasted_iota(jnp.int32, (1, SUBLANES, D), 1)
    b3 = _tile_prefix(g.reshape(sb, r, D), sub)
    b = b3.reshape(rows, D)
    b_last3 = b3[:, r - 1:r, :]
    qe3 = (qr.reshape(sb, r, D) * jnp.exp2(b3)).astype(bf16)
    khat3 = (kr.reshape(sb, r, D) * jnp.exp2(b_last3 - b3)).astype(bf16)
    hs = (1, 2, 4)
    levels = [_level_operands(qr, kr, b, hsz) for hsz in hs]
    qr_b = qr.astype(bf16)
    kr_b = kr.astype(bf16)
    vb = vr.astype(bf16)
    vb3 = vb.reshape(sb, r, D)
    b_last = b_last3.reshape(sb, D)

    outs = []
    for hd in range(REC_HEADS):
        col = slice(hd * DK, (hd + 1) * DK)
        s_old = s0_ref[:, hd]
        att = _intra_scores(qr_b, kr_b, levels, hs, masks_ref, col)
        o_intra = _dot(att.astype(bf16), vb[:, col])
        o_inter = jnp.einsum('bqk,bkv->bqv', qe3[:, :, col], s_old.astype(bf16),
                             preferred_element_type=f32)
        upd = jnp.einsum('btk,btv->bkv', khat3[:, :, col], vb3[:, :, col],
                         preferred_element_type=f32)
        bl = b_last[:, col]
        for i in range(sb):
            s1_ref[i, hd] = _decay_column(bl, i) * s_old[i] + upd[i]
        o = o_inter.reshape(rows, DK) + o_intra
        outs.append(_head_norm_gate(o, hn[:, col], rg[:, col]))
    o_rec = jnp.concatenate(outs, axis=1)
    mixed = a_ref[...].reshape(rows, D) + _sigmoid(gr) * o_rec
    y = _dot(mixed.astype(bf16), _unpack_rows(wo_ref[0]))
    o_ref[...] = x + ga * y.reshape(sb, r, D)


def _rec_sample_call(x, mod, nw, w_rec, lb, hn, a_mix, w_out, masks, s0, s1_all, layer):
    nb, r, _ = x.shape
    wcols = w_rec.shape[-1]
    blk3 = lambda i: (i, 0, 0)
    return pl.pallas_call(
        _rec_sample_kernel,
        out_shape=(jax.ShapeDtypeStruct((nb, r, D), f32),
                   jax.ShapeDtypeStruct(s1_all.shape, f32)),
        grid=(nb // SB_REC,),
        input_output_aliases={10: 1},
        in_specs=[
            pl.BlockSpec((SB_REC, r, D), blk3),
            pl.BlockSpec((1, SB_REC, 1, 6 * D), lambda i: (layer, i, 0, 0)),
            _resident((1, 1, D), lambda i: (layer, 0, 0)),
            _resident((1, D // 2, wcols), lambda i: (layer, 0, 0)),
            _resident((1, 1, D), lambda i: (layer, 0, 0)),
            _resident((1, 1, D), lambda i: (layer, 0, 0)),
            pl.BlockSpec((SB_REC, r, D), blk3),
            _resident((1, D // 2, D), lambda i: (layer, 0, 0)),
            _resident(masks.shape, lambda i: (0, 0, 0)),
            pl.BlockSpec((None, SB_REC, REC_HEADS, DK, DK), lambda i: (layer, i, 0, 0, 0)),
            pl.BlockSpec(memory_space=pl.ANY),
        ],
        out_specs=(pl.BlockSpec((SB_REC, r, D), blk3),
                   pl.BlockSpec((None, SB_REC, REC_HEADS, DK, DK),
                                lambda i: (layer, i, 0, 0, 0))),
        compiler_params=_cparams(1),
        name="rec_sample",
    )(x, mod, nw, w_rec, lb, hn, a_mix, w_out, masks, s0, s1_all)


def _alibi_slopes():
    return 2.0 ** (-8.0 * jnp.arange(1, N_HEADS + 1, dtype=f32) / N_HEADS)


def _prompt_bias():
    t = np.arange(WINDOW)[:, None]
    c = np.arange(2 * WINDOW)[None, :]
    dist = WINDOW + t - c
    valid = (dist >= 0) & (dist <= WINDOW)
    valid = np.stack([valid, valid & (c >= WINDOW)])
    slopes = _alibi_slopes()[None, :, None, None]
    return jnp.where(jnp.asarray(valid)[:, None], -slopes * jnp.asarray(dist, f32), MASK_VALUE)


def _sample_bias(r, w, sb):
    t = np.arange(r)[:, None]
    c = np.arange(w)[None, :]
    dist_c = w + t - c
    valid_c = (dist_c >= 0) & (dist_c <= WINDOW)
    slopes = _alibi_slopes()
    bias_c = jnp.where(jnp.asarray(valid_c)[None], -slopes[:, None, None] * jnp.asarray(dist_c, f32),
                       MASK_VALUE)
    bq = np.arange(sb)[:, None, None, None]
    tq = np.arange(r)[None, :, None, None]
    bk = np.arange(sb)[None, None, :, None]
    tk = np.arange(r)[None, None, None, :]
    dist_n = np.broadcast_to(tq - tk, (sb, r, sb, r)).reshape(sb, r, sb * r)
    valid_n = np.broadcast_to((bq == bk) & (tq >= tk), (sb, r, sb, r)).reshape(sb, r, sb * r)
    bias_n = jnp.where(jnp.asarray(valid_n)[None],
                       -slopes[:, None, None, None] * jnp.asarray(dist_n, f32), MASK_VALUE)
    return bias_c, bias_n


def _level_masks(rows, hs):
    t = np.arange(rows)[:, None]
    s = np.arange(rows)[None, :]
    out = [t == s]
    for hsz in hs:
        same = (t // (2 * hsz)) == (s // (2 * hsz))
        out.append(same & ((t % (2 * hsz)) >= hsz) & ((s % (2 * hsz)) < hsz))
    return jnp.asarray(np.stack(out), f32)


def kernel(x_prompt, x_sample, cache_win_k, cache_win_v, state_hgrn, c_prompt, c_sample,
           norm_mix, norm_mlp, w_ada, b_ada, w_in, attn_sinks, hgrn_lb_logits, hgrn_norm,
           w_out, w_up, w_down, final_norm):
    depth = w_in.shape[0]
    nb_p, seq, _ = x_prompt.shape
    nb_s, dec, _ = x_sample.shape
    w_buf = cache_win_k.shape[2]
    kvw = N_KV * HD
    assert seq % T_ATT == 0 and seq % T_REC == 0 and seq % T_MLP == 0
    assert T_REC % (2 * CHUNK) == 0
    assert dec == SUBLANES and w_buf == WINDOW
    assert nb_s % SB_ATT == 0 and nb_s % SB_REC == 0 and nb_s % SB_MLP == 0

    pad = (-(nb_s + nb_p)) % SUBLANES
    c_all = jnp.concatenate([c_sample, c_prompt, jnp.zeros((pad, D), f32)], axis=0)
    n_c = c_all.shape[0]
    n_mod = 6 * D
    mod_cols = 1536
    mod = pl.pallas_call(
        _mod_kernel,
        out_shape=jax.ShapeDtypeStruct((depth, n_c, n_mod), f32),
        grid=(depth, n_mod // mod_cols),
        in_specs=[
            _resident((n_c, D), lambda l, j: (0, 0)),
            pl.BlockSpec((1, D, mod_cols), lambda l, j: (l, 0, j)),
            pl.BlockSpec((1, 1, mod_cols), lambda l, j: (l, 0, j)),
        ],
        out_specs=pl.BlockSpec((1, n_c, mod_cols), lambda l, j: (l, 0, j)),
        compiler_params=_cparams(2),
        name="adaln_mod",
    )(c_all, w_ada, b_ada.reshape(depth, 1, n_mod))
    mod = mod.reshape(depth, n_c, 1, n_mod)
    prompt_row0 = nb_s

    lb_all = pl.pallas_call(
        _lb_kernel,
        out_shape=jax.ShapeDtypeStruct(hgrn_lb_logits.shape, f32),
        name="hgrn_lower_bound",
    )(hgrn_lb_logits).reshape(depth, 1, D)

    v_end = D + 2 * kvw
    rec_end = v_end + 4 * D
    w_attn, w_rec = _pack_rows(w_in, col_groups=(
        ((0, v_end), (rec_end, rec_end + D)),
        ((v_end, rec_end), (rec_end + D, w_in.shape[-1]))))
    w_out_b = _pack_rows(w_out)
    w_up_b = _pack_rows(w_up)
    w_dn_b = _pack_rows(w_down)

    nmix = norm_mix.reshape(depth, 1, D)
    nmlp = norm_mlp.reshape(depth, 1, D)
    hnorm = hgrn_norm.reshape(depth, 1, D)
    fnorm = final_norm.reshape(1, D)

    bias_p = _prompt_bias()
    bias_c, bias_n = _sample_bias(dec, w_buf, SB_ATT)
    masks_p = _level_masks(CHUNK, (1, 2, 4, 8, 16, 32, 64))
    masks_s = _level_masks(SB_REC * dec, (1, 2, 4))

    ck_all = cache_win_k.reshape(depth, nb_s, w_buf, kvw)
    cv_all = cache_win_v.reshape(depth, nb_s, w_buf, kvw)

    xp, xs = x_prompt, x_sample
    pk, pv, ps = [], [], []
    sk = jnp.zeros(ck_all.shape, f32)
    sv = jnp.zeros(cv_all.shape, f32)
    ss = jnp.zeros(state_hgrn.shape, f32)
    for l in range(depth):
        final = l == depth - 1
        a_p, k_p, v_p = _attn_prompt_call(xp, mod, prompt_row0, nmix, w_attn, attn_sinks,
                                          bias_p, l)
        xp, s_p = _rec_prompt_call(xp, mod, prompt_row0, nmix, w_rec, lb_all, hnorm, a_p,
                                   w_out_b, masks_p, l)
        xp = _mlp_prompt(xp, mod, prompt_row0, nmlp, w_up_b, w_dn_b, fnorm, l, final)

        a_s, sk, sv = _attn_sample_call(xs, mod, nmix, w_attn, attn_sinks, ck_all, cv_all,
                                        bias_c, bias_n, sk, sv, l)
        xs, ss = _rec_sample_call(xs, mod, nmix, w_rec, lb_all, hnorm, a_s, w_out_b,
                                  masks_s, state_hgrn, ss, l)
        xs = _mlp_call(xs, mod, 0, nmlp, w_up_b, w_dn_b, fnorm, l, final, SB_MLP)

        pk.append(k_p.reshape(nb_p, w_buf, N_KV, HD))
        pv.append(v_p.reshape(nb_p, w_buf, N_KV, HD))
        ps.append(s_p)

    out_kv = (depth, nb_s, w_buf, N_KV, HD)
    return (xp, xs, jnp.stack(pk), jnp.stack(pv), jnp.stack(ps),
            sk.reshape(out_kv), sv.reshape(out_kv), ss)


def _mlp_prompt(x, mod, mod_row0, nw, wup, wdn, fn, layer, final):
    b, l, _ = x.shape
    return pl.pallas_call(
        functools.partial(_mlp_kernel, final=final),
        out_shape=jax.ShapeDtypeStruct(x.shape, f32),
        grid=(b, l // T_MLP),
        in_specs=[
            pl.BlockSpec((1, T_MLP, D), lambda i, t: (i, t, 0)),
            pl.BlockSpec((1, 1, 1, 6 * D), lambda i, t: (layer, mod_row0 + i, 0, 0)),
            _resident((1, 1, D), lambda i, t: (layer, 0, 0)),
            _resident((1, D // 2, D_FF), lambda i, t: (layer, 0, 0)),
            _resident((1, D_FF // 2, D), lambda i, t: (layer, 0, 0)),
            _resident((1, D), lambda i, t: (0, 0)),
        ],
        out_specs=pl.BlockSpec((1, T_MLP, D), lambda i, t: (i, t, 0)),
        compiler_params=_cparams(2),
        name="mlp_prompt",
    )(x, mod, nw, wup, wdn, fn)
```
